```python
import jax, jax.numpy as jnp
from jax import lax
import numpy as np

D_MODEL = 1024
BATCH = 8
SEQ = 2048
DEPTH = 1
DEC_BATCH = 32
DEC_SEQ = 1
PAST_LEN = 16384
PAGE_SIZE = 128

HD = 64
NSA_H = D_MODEL // 128
NSA_KV = 2
NSA_R = NSA_H // NSA_KV
CMP_BLOCK = 64
N_SELECT = 16
WINDOW = 512
FOX_H = D_MODEL // 128
MEM_H = 4
N_MEM = 256
BR_W = NSA_H * HD
MEM_HD = BR_W // MEM_H
N_BRANCH = 3
ROT_DIM = HD // 4
ROPE_THETA = 500000.0
Q_BLOCK = 128
SLC_Q_BLOCK = 64
EPS = 1e-6
NEG = -1e30
BIG = 1e4
TINY = 1e-30
KV_W = NSA_KV * HD
COL_SIZES = (
    BR_W,
    2 * KV_W,
    2 * KV_W,
    2 * KV_W,
    3 * NSA_H,
    BR_W,
    BR_W,
    2 * BR_W,
    FOX_H,
    BR_W,
    BR_W,
    BR_W,
    N_BRANCH * D_MODEL,
)
N_IN = sum(COL_SIZES)

kernel_name = 'nsa_fox_memory_gated_hybrid_step'


def _rmsnorm(x, g):
    xf = x.astype(jnp.float32)
    y = xf * lax.rsqrt(jnp.mean(xf * xf, axis=-1, keepdims=True) + EPS)
    return (y * g.astype(jnp.float32)).astype(x.dtype)


def _rope_partial(x, pos):
    half = ROT_DIM // 2
    inv_freq = ROPE_THETA ** (-jnp.arange(half, dtype=jnp.float32) / half)
    ang = pos[:, None] * inv_freq[None, :]
    cos = jnp.cos(ang)[:, None, :]
    sin = jnp.sin(ang)[:, None, :]
    xr = x[..., :ROT_DIM].astype(jnp.float32)
    x1, x2 = xr[..., :half], xr[..., half:]
    rot = jnp.concatenate([x1 * cos - x2 * sin, x2 * cos + x1 * sin], axis=-1)
    return jnp.concatenate([rot.astype(x.dtype), x[..., ROT_DIM:]], axis=-1)


def _masked_softmax(s, mask, axis=-1):
    s = jnp.where(mask, s.astype(jnp.float32), NEG)
    m = jnp.max(s, axis=axis, keepdims=True)
    e = jnp.where(mask, jnp.exp(s - m), 0.0)
    return e / jnp.maximum(jnp.sum(e, axis=axis, keepdims=True), TINY)


def _project(xn, w_in, b_fgate, pos):
    B, T = xn.shape[:2]
    h = jnp.einsum('btd,dn->btn', xn, w_in)
    splits = np.cumsum(np.array(COL_SIZES))[:-1].tolist()
    (nq, ncmp, nslc, nwin, ngate, nsilu, fq, fkv, flog, fsilu, mq, msilu, bgate) = jnp.split(h, splits, axis=-1)

    def kv_rows(a):
        a = a.reshape(B, T, 2, NSA_KV, HD)
        return jnp.stack([_rope_partial(a[:, :, 0], pos), a[:, :, 1]], axis=2)

    q = _rope_partial(nq.reshape(B, T, NSA_H, HD), pos)
    logf = jax.nn.log_sigmoid(flog.astype(jnp.float32) + b_fgate.astype(jnp.float32))
    return (q, kv_rows(ncmp), kv_rows(nslc), kv_rows(nwin), ngate, nsilu,
            fq.reshape(B, T, FOX_H, HD), fkv.reshape(B, T, 2, FOX_H, HD), logf, fsilu,
            mq.reshape(B, T, MEM_H, MEM_HD), msilu, bgate)


def _compress(rows, w1, w2, pe):
    r = rows.reshape(rows.shape[:-3] + (-1, CMP_BLOCK) + rows.shape[-2:])
    r = r + pe[:, None, :].astype(r.dtype)
    hid = jax.nn.silu(jnp.einsum('...nlgd,lde->...nge', r, w1))
    return jnp.einsum('...nge,ef->...ngf', hid, w2)


def _nsa_combine(gates, o_c, o_s, o_w):
    B, T = gates.shape[:2]
    g = jax.nn.sigmoid(gates.astype(jnp.float32)).reshape(B, T, 3, NSA_KV, NSA_R, 1)
    o = g[:, :, 0] * o_c + g[:, :, 1] * o_s + g[:, :, 2] * o_w
    return o.astype(o_c.dtype).reshape(B, T, BR_W)


def _nsa_prompt(q, cmp, slc, win, gates, w1, w2, pe):
    B, T = q.shape[:2]
    nb = T // CMP_BLOCK
    scale = HD ** -0.5
    qg = q.reshape(B, T, NSA_KV, NSA_R, HD)
    tpos = jnp.arange(T)
    blk = jnp.arange(nb)
    kcb = _compress(cmp[:, :, 0], w1[0], w2[0], pe[0])
    vcb = _compress(cmp[:, :, 1], w1[1], w2[1], pe[1])
    s_c = jnp.einsum('btgrd,bngd->btgrn', qg, kcb) * scale
    readable = (blk[None, :] + 1) * CMP_BLOCK - 1 <= tpos[:, None]
    p_c = _masked_softmax(s_c, readable[None, :, None, None, :])
    o_c = jnp.einsum('btgrn,bngd->btgrd', p_c.astype(vcb.dtype), vcb)
    cur = (tpos // CMP_BLOCK)[:, None]
    imp = jnp.sum(p_c, axis=3)
    sel_score = jnp.where((blk[None, :] < cur)[None, :, None, :], imp,
                          jnp.where((blk[None, :] == cur)[None, :, None, :], BIG, NEG))
    n_sel = min(N_SELECT, nb)
    top_v, top_i = lax.top_k(sel_score, n_sel)
    top_ok = top_v > 0.5 * NEG
    slc_b = slc.reshape(B, nb, CMP_BLOCK, 2, NSA_KV, HD)
    bi = jnp.arange(B)[:, None, None, None]
    gi = jnp.arange(NSA_KV)[None, None, :, None]

    def sel_block(i):
        start = i * SLC_Q_BLOCK
        qb = lax.dynamic_slice_in_dim(qg, start, SLC_Q_BLOCK, axis=1)
        idx = lax.dynamic_slice_in_dim(top_i, start, SLC_Q_BLOCK, axis=1)
        ok = lax.dynamic_slice_in_dim(top_ok, start, SLC_Q_BLOCK, axis=1)
        kv = slc_b[bi, idx, :, :, gi]
        s = jnp.einsum('bqgrd,bqgkld->bqgrkl', qb, kv[..., 0, :]) * scale
        qpos = start + jnp.arange(SLC_Q_BLOCK)
        spos = idx[..., None] * CMP_BLOCK + jnp.arange(CMP_BLOCK)
        m = ok[..., None] & (spos <= qpos[None, :, None, None, None])
        p = _masked_softmax(s, m[:, :, :, None], axis=(-2, -1))
        return jnp.einsum('bqgrkl,bqgkld->bqgrd', p.astype(kv.dtype), kv[..., 1, :])

    o_s = lax.map(sel_block, jnp.arange(T // SLC_Q_BLOCK))
    o_s = jnp.moveaxis(o_s, 0, 1).reshape(B, T, NSA_KV, NSA_R, HD)
    span = Q_BLOCK + WINDOW
    win_p = jnp.pad(win, ((0, 0), (WINDOW, 0), (0, 0), (0, 0), (0, 0)))

    def win_block(i):
        start = i * Q_BLOCK
        qb = lax.dynamic_slice_in_dim(qg, start, Q_BLOCK, axis=1)
        kv = lax.dynamic_slice_in_dim(win_p, start, span, axis=1)
        s = jnp.einsum('bqgrd,bsgd->bqgrs', qb, kv[:, :, 0]) * scale
        qpos = start + jnp.arange(Q_BLOCK)
        spos = start - WINDOW + jnp.arange(span)
        m = (spos[None, :] <= qpos[:, None]) & (spos[None, :] >= qpos[:, None] - WINDOW) & (spos[None, :] >= 0)
        p = _masked_softmax(s, m[None, :, None, None, :])
        return jnp.einsum('bqgrs,bsgd->bqgrd', p.astype(kv.dtype), kv[:, :, 1])

    o_w = lax.map(win_block, jnp.arange(T // Q_BLOCK))
    o_w = jnp.moveaxis(o_w, 0, 1).reshape(B, T, NSA_KV, NSA_R, HD)
    return _nsa_combine(gates, o_c, o_s, o_w)


def _nsa_sample(q, cmp, slc, win, gates, pool_cmp, pool_slc, win_state, page_table, l, w1, w2, pe):
    DB, S1 = q.shape[:2]
    past = page_table.shape[1] * PAGE_SIZE
    nb = past // CMP_BLOCK
    scale = HD ** -0.5
    qg = q.reshape(DB, S1, NSA_KV, NSA_R, HD)
    qpos = past + jnp.arange(S1)
    causal = jnp.arange(S1)[None, :] <= jnp.arange(S1)[:, None]
    kc_rows = pool_cmp[l, page_table, :, 0].reshape(DB, past, NSA_KV, HD)
    vc_rows = pool_cmp[l, page_table, :, 1].reshape(DB, past, NSA_KV, HD)
    kcb = _compress(kc_rows, w1[0], w2[0], pe[0])
    vcb = _compress(vc_rows, w1[1], w2[1], pe[1])
    s_c = jnp.einsum('bqgrd,bngd->bqgrn', qg, kcb) * scale
    p_c = jax.nn.softmax(s_c.astype(jnp.float32), axis=-1)
    o_c = jnp.einsum('bqgrn,bngd->bqgrd', p_c.astype(vcb.dtype), vcb)
    imp = jnp.sum(p_c, axis=3)
    k1 = min(N_SELECT - 1, nb)
    _, idx = lax.top_k(imp, k1)
    bpp = PAGE_SIZE // CMP_BLOCK
    phys = page_table[jnp.arange(DB)[:, None, None, None], idx // bpp]
    rows = (idx % bpp)[..., None] * CMP_BLOCK + jnp.arange(CMP_BLOCK)
    gi = jnp.arange(NSA_KV)[None, None, :, None, None]
    kv_sel = pool_slc[l, phys[..., None], rows, :, gi]
    n_past = k1 * CMP_BLOCK
    s_past = (jnp.einsum('bqgrd,bqgkld->bqgrkl', qg, kv_sel[..., 0, :]) * scale).reshape(DB, S1, NSA_KV, NSA_R, n_past)
    s_cur = jnp.einsum('bqgrd,bsgd->bqgrs', qg, slc[:, :, 0]) * scale
    m = jnp.concatenate([jnp.ones((S1, n_past), bool), causal], axis=-1)
    p = _masked_softmax(jnp.concatenate([s_past.astype(jnp.float32), s_cur.astype(jnp.float32)], axis=-1),
                        m[None, :, None, None, :]).astype(q.dtype)
    o_s = (jnp.einsum('bqgrkl,bqgkld->bqgrd', p[..., :n_past].reshape(DB, S1, NSA_KV, NSA_R, k1, CMP_BLOCK), kv_sel[..., 1, :])
           + jnp.einsum('bqgrs,bsgd->bqgrd', p[..., n_past:], slc[:, :, 1]))
    wb = win_state.shape[1]
    kv_w = jnp.concatenate([win_state, win], axis=1)
    spos = past - wb + jnp.arange(wb + S1)
    mw = (spos[None, :] <= qpos[:, None]) & (spos[None, :] >= qpos[:, None] - WINDOW)
    s_w = jnp.einsum('bqgrd,bsgd->bqgrs', qg, kv_w[:, :, 0]) * scale
    p_w = _masked_softmax(s_w, mw[None, :, None, None, :])
    o_w = jnp.einsum('bqgrs,bsgd->bqgrd', p_w.astype(kv_w.dtype), kv_w[:, :, 1])
    return _nsa_combine(gates, o_c, o_s, o_w), kv_w[:, S1:]


def _fox_prompt(q, kv, logf):
    B, T = q.shape[:2]
    scale = HD ** -0.5
    k, v = kv[:, :, 0], kv[:, :, 1]
    c = jnp.swapaxes(jnp.cumsum(logf, axis=1), 1, 2)
    kpos = jnp.arange(T)

    def block(i):
        start = i * Q_BLOCK
        qb = lax.dynamic_slice_in_dim(q, start, Q_BLOCK, axis=1)
        cb = lax.dynamic_slice_in_dim(c, start, Q_BLOCK, axis=2)
        s = jnp.einsum('bqhd,bshd->bhqs', qb, k).astype(jnp.float32) * scale + cb[..., :, None] - c[..., None, :]
        qpos = start + jnp.arange(Q_BLOCK)
        p = _masked_softmax(s, (kpos[None, :] <= qpos[:, None])[None, None])
        return jnp.einsum('bhqs,bshd->bqhd', p.astype(v.dtype), v)

    o = lax.map(block, jnp.arange(T // Q_BLOCK))
    return jnp.moveaxis(o, 0, 1).reshape(B, T, FOX_H * HD)


def _fox_sample(q, kv, logf, pool_kv, pool_logf, page_table, l):
    DB, S1 = q.shape[:2]
    past = page_table.shape[1] * PAGE_SIZE
    scale = HD ** -0.5
    k_past = pool_kv[l, page_table, :, 0].reshape(DB, past, FOX_H, HD)
    v_past = pool_kv[l, page_table, :, 1].reshape(DB, past, FOX_H, HD)
    lf_past = pool_logf[l, page_table].reshape(DB, past, FOX_H).astype(jnp.float32)
    r_past = jnp.swapaxes(lax.cumsum(lf_past, axis=1, reverse=True) - lf_past, 1, 2)
    a_new = jnp.swapaxes(jnp.cumsum(logf, axis=1), 1, 2)
    s_p = jnp.einsum('bqhd,bshd->bhqs', q, k_past).astype(jnp.float32) * scale + a_new[..., :, None] + r_past[..., None, :]
    s_n = jnp.einsum('bqhd,bshd->bhqs', q, kv[:, :, 0]).astype(jnp.float32) * scale + a_new[..., :, None] - a_new[..., None, :]
    causal = jnp.arange(S1)[None, :] <= jnp.arange(S1)[:, None]
    m = jnp.concatenate([jnp.ones((S1, past), bool), causal], axis=-1)
    p = _masked_softmax(jnp.concatenate([s_p, s_n], axis=-1), m[None, None]).astype(q.dtype)
    o = (jnp.einsum('bhqs,bshd->bqhd', p[..., :past], v_past)
         + jnp.einsum('bhqs,bshd->bqhd', p[..., past:], kv[:, :, 1]))
    return o.reshape(DB, S1, FOX_H * HD)


def _mem_kv(mem, g_mem, w_mem_kv):
    B, M = mem.shape[:2]
    mn = _rmsnorm(mem, g_mem)
    return jnp.einsum('bmd,dn->bmn', mn, w_mem_kv).reshape(B, M, 2, MEM_H, MEM_HD)


def _mem_attn(q, mkv):
    B, T = q.shape[:2]
    s = jnp.einsum('bthd,bmhd->bhtm', q, mkv[:, :, 0]).astype(jnp.float32) * (MEM_HD ** -0.5)
    p = jax.nn.softmax(s, axis=-1)
    return jnp.einsum('bhtm,bmhd->bthd', p.astype(mkv.dtype), mkv[:, :, 1]).reshape(B, T, BR_W)


def _merge(o_nsa, o_fox, o_mem, nsilu, fsilu, msilu, bgate, w_branch, w_out, g_post):
    B, T = o_nsa.shape[:2]
    br = jnp.stack([o_nsa * jax.nn.silu(nsilu), o_fox * jax.nn.silu(fsilu), o_mem * jax.nn.silu(msilu)], axis=2)
    z = jnp.einsum('btkw,kwd->btkd', br, w_branch)
    g = jax.nn.sigmoid(bgate.reshape(B, T, N_BRANCH, D_MODEL))
    merged = jnp.sum(g * z, axis=2)
    return _rmsnorm(jnp.einsum('btd,de->bte', merged, w_out), g_post)


def setup_inputs(seed: int = 0) -> dict:
    key = jax.random.key(seed)
    ks = jax.random.split(key, 24)
    f32 = jnp.float32
    n_pages = PAST_LEN // PAGE_SIZE
    n_used = DEC_BATCH * n_pages
    n_pool = n_used + max(1, n_used // 4)
    win_buf = min(WINDOW, PAST_LEN)

    def nrm(k, shape, scale=1.0):
        return scale * jax.random.normal(k, shape, f32)

    perm = jax.random.permutation(ks[8], n_pool)[:n_used]
    return {
        'x_prompt': nrm(ks[0], (BATCH, SEQ, D_MODEL)),
        'x_sample': nrm(ks[1], (DEC_BATCH, DEC_SEQ, D_MODEL)),
        'cache_nsa_cmp': nrm(ks[2], (DEPTH, n_pool, PAGE_SIZE, 2, NSA_KV, HD)),
        'cache_nsa_slc': nrm(ks[3], (DEPTH, n_pool, PAGE_SIZE, 2, NSA_KV, HD)),
        'state_nsa_win': nrm(ks[4], (DEPTH, DEC_BATCH, win_buf, 2, NSA_KV, HD)),
        'cache_fox_kv': nrm(ks[5], (DEPTH, n_pool, PAGE_SIZE, 2, FOX_H, HD)),
        'cache_fox_logf': jax.nn.log_sigmoid(3.0 + nrm(ks[6], (DEPTH, n_pool, PAGE_SIZE, FOX_H))),
        'cache_mem_kv': nrm(ks[7], (DEPTH, DEC_BATCH, N_MEM, 2, MEM_H, MEM_HD)),
        'page_table': perm.reshape(DEC_BATCH, n_pages).astype(jnp.int32),
        'mem_prompt': nrm(ks[9], (BATCH, N_MEM, D_MODEL)),
        'g_pre': 1.0 + nrm(ks[10], (DEPTH, D_MODEL), 0.1),
        'g_post': 1.0 + nrm(ks[11], (DEPTH, D_MODEL), 0.1),
        'g_mem': 1.0 + nrm(ks[12], (DEPTH, D_MODEL), 0.1),
        'w_in': nrm(ks[13], (DEPTH, D_MODEL, N_IN), D_MODEL ** -0.5),
        'b_fgate': jax.random.uniform(ks[14], (DEPTH, FOX_H), f32, 1.0, 4.0),
        'w_mem_kv': nrm(ks[15], (DEPTH, D_MODEL, 2 * BR_W), D_MODEL ** -0.5),
        'w_cmp1': nrm(ks[16], (DEPTH, 2, CMP_BLOCK, HD, HD), (CMP_BLOCK * HD) ** -0.5),
        'w_cmp2': nrm(ks[17], (DEPTH, 2, HD, HD), HD ** -0.5),
        'pe_cmp': nrm(ks[18], (DEPTH, 2, CMP_BLOCK, HD), 0.1),
        'w_branch': nrm(ks[19], (DEPTH, N_BRANCH, BR_W, D_MODEL), BR_W ** -0.5),
        'w_out': nrm(ks[20], (DEPTH, D_MODEL, D_MODEL), D_MODEL ** -0.5),
    }


def reference(x_prompt, x_sample, cache_nsa_cmp, cache_nsa_slc, state_nsa_win, cache_fox_kv,
              cache_fox_logf, cache_mem_kv, page_table, mem_prompt, g_pre, g_post, g_mem, w_in,
              b_fgate, w_mem_kv, w_cmp1, w_cmp2, pe_cmp, w_branch, w_out):
    T = x_prompt.shape[1]
    S1 = x_sample.shape[1]
    past = page_table.shape[1] * PAGE_SIZE
    pos_p = jnp.arange(T, dtype=jnp.float32)
    pos_s = jnp.arange(S1, dtype=jnp.float32) + jnp.float32(past)
    xp, xs = x_prompt, x_sample
    p_cmp, p_slc, p_win, p_fkv, p_flog, p_mem = [], [], [], [], [], []
    s_cmp, s_slc, s_win, s_fkv, s_flog = [], [], [], [], []
    for l in range(DEPTH):
        (q, cmp, slc, win, ngate, nsilu, fq, fkv, flog, fsilu, mq, msilu, bgate) = _project(
            _rmsnorm(xp, g_pre[l]), w_in[l], b_fgate[l], pos_p)
        o_nsa = _nsa_prompt(q, cmp, slc, win, ngate, w_cmp1[l], w_cmp2[l], pe_cmp[l])
        o_fox = _fox_prompt(fq, fkv, flog)
        mkv = _mem_kv(mem_prompt, g_mem[l], w_mem_kv[l])
        o_mem = _mem_attn(mq, mkv)
        xp = xp + _merge(o_nsa, o_fox, o_mem, nsilu, fsilu, msilu, bgate, w_branch[l], w_out[l], g_post[l])
        p_cmp.append(cmp)
        p_slc.append(slc)
        p_win.append(win[:, T - min(WINDOW, T):])
        p_fkv.append(fkv)
        p_flog.append(flog)
        p_mem.append(mkv)
        (q, cmp, slc, win, ngate, nsilu, fq, fkv, flog, fsilu, mq, msilu, bgate) = _project(
            _rmsnorm(xs, g_pre[l]), w_in[l], b_fgate[l], pos_s)
        o_nsa, new_win = _nsa_sample(q, cmp, slc, win, ngate, cache_nsa_cmp, cache_nsa_slc, state_nsa_win[l],
                                     page_table, l, w_cmp1[l], w_cmp2[l], pe_cmp[l])
        o_fox = _fox_sample(fq, fkv, flog, cache_fox_kv, cache_fox_logf, page_table, l)
        o_mem = _mem_attn(mq, cache_mem_kv[l])
        xs = xs + _merge(o_nsa, o_fox, o_mem, nsilu, fsilu, msilu, bgate, w_branch[l], w_out[l], g_post[l])
        s_cmp.append(cmp)
        s_slc.append(slc)
        s_win.append(new_win)
        s_fkv.append(fkv)
        s_flog.append(flog)
    return (xp, xs, jnp.stack(p_cmp), jnp.stack(p_slc), jnp.stack(p_win), jnp.stack(p_fkv), jnp.stack(p_flog), jnp.stack(p_mem), jnp.stack(s_cmp), jnp.stack(s_slc), jnp.stack(s_win), jnp.stack(s_fkv), jnp.stack(s_flog))
```

```python
import functools

import numpy as np
import jax
import jax.numpy as jnp
from jax import lax
from jax.experimental import pallas as pl
from jax.experimental.pallas import tpu as pltpu

HD = 64
NSA_H = 8
NSA_KV = 2
NSA_R = NSA_H // NSA_KV
CMP_BLOCK = 64
N_SELECT = 16
WINDOW = 512
FOX_H = 8
MEM_H = 4
MEM_HD = 128
BR_W = 512
ROT_DIM = HD // 4
ROT_HALF = ROT_DIM // 2
ROPE_THETA = 500000.0
PAGE_SIZE = 128
EPS = 1e-6
NEG = -1e30
BIG = 1e4
TINY = 1e-30
KV_W = NSA_KV * HD
SCALE = HD ** -0.5

LANES = 128
SUBLANES = 8
VMEM_LIMIT = 56 * 1024 * 1024

_COLS = dict(nq=(0, 512), ncmp=(512, 768), nslc=(768, 1024), nwin=(1024, 1280), ngate=(1280, 1304),
             nsilu=(1304, 1816), fq=(1816, 2328), fkv=(2328, 3352), flog=(3352, 3360), fsilu=(3360, 3872),
             mq=(3872, 4384), msilu=(4384, 4896), bgate=(4896, 7968))
N_IN = 7968
N_IN_PAD = 8064

F32 = jnp.float32
BF16 = jnp.bfloat16


def _nt_dot(a, b):
    return lax.dot_general(a, b, (((1,), (1,)), ((), ())), preferred_element_type=F32)


def _nn_dot(a, b):
    return lax.dot_general(a, b, (((1,), (0,)), ((), ())), preferred_element_type=F32)


def _rmsnorm_rows(x, g):
    var = jnp.mean(x * x, axis=-1, keepdims=True)
    return x * lax.rsqrt(var + EPS) * g


def _silu(x):
    return x * (1.0 / (1.0 + jnp.exp(-x)))


def _sigmoid(x):
    return 1.0 / (1.0 + jnp.exp(-x))


def _rope_tables(pos):
    inv_freq = ROPE_THETA ** (-jnp.arange(ROT_HALF, dtype=F32) / ROT_HALF)
    ang = pos[:, None] * inv_freq[None, :]
    cos, sin = jnp.cos(ang), jnp.sin(ang)
    t = pos.shape[0]
    one = jnp.ones((t, HD - ROT_DIM), F32)
    zero = jnp.zeros((t, HD - ROT_HALF), F32)
    c_head = jnp.concatenate([cos, cos, one], axis=1)
    s_hi = jnp.concatenate([-sin, zero], axis=1)
    s_lo = jnp.concatenate([jnp.zeros((t, ROT_HALF), F32), sin, jnp.zeros((t, HD - ROT_DIM), F32)], axis=1)
    tile2 = lambda a: jnp.concatenate([a, a], axis=1)
    return tile2(c_head), tile2(s_hi), tile2(s_lo), cos, sin


def _rope_lanes(x, c, s_hi, s_lo):
    k = x.shape[1] // LANES
    if k > 1:
        c = jnp.concatenate([c] * k, axis=1)
        s_hi = jnp.concatenate([s_hi] * k, axis=1)
        s_lo = jnp.concatenate([s_lo] * k, axis=1)
    n = x.shape[1]
    up = pltpu.roll(x, n - ROT_HALF, axis=1)
    dn = pltpu.roll(x, ROT_HALF, axis=1)
    return x * c + up * s_hi + dn * s_lo


_QGROUPS = ("nq", "nsilu", "fq", "fsilu", "mq", "msilu")
_KVGROUPS = (("ncmp", 256), ("nslc", 256), ("nwin", 256), ("fkv", 1024), ("flog", 8))
NQ_ROWS = 6 * 512 + 3072 + LANES
NKV_ROWS = 256 * 3 + 1024 + 8
PROJ_TM = 512


def _proj_prompt_kernel(x_ref, g_ref, wq_ref, wkv_ref, c_ref, shi_ref, slo_ref, cosT_ref, sinT_ref, bf_ref,
                        nq_ref, nsilu_ref, fq_ref, fsilu_ref, mq_ref, msilu_ref, bgate_ref, ngate_ref,
                        cmpT_ref, slcT_ref, winT_ref, fkvT_ref, flogT_ref):
    xb = _rmsnorm_rows(x_ref[0], g_ref[...]).astype(BF16)
    outs = (nq_ref, nsilu_ref, fq_ref, fsilu_ref, mq_ref, msilu_ref)
    for i, o_ref in enumerate(outs):
        h = _nt_dot(xb, wq_ref[i * 512:(i + 1) * 512, :])
        if i == 0:
            h = _rope_lanes(h, c_ref[...], shi_ref[...], slo_ref[...])
        if i in (0, 2):
            h = h * SCALE
        o_ref[0] = h.astype(o_ref.dtype)
    off = 6 * 512
    for j in range(3072 // 512):
        h = _nt_dot(xb, wq_ref[off + j * 512: off + (j + 1) * 512, :])
        bgate_ref[0, :, j * 512:(j + 1) * 512] = h.astype(bgate_ref.dtype)
    off += 3072
    ngate_ref[0] = _nt_dot(xb, wq_ref[off:off + LANES, :])
    cosT, sinT = cosT_ref[...], sinT_ref[...]
    roff = 0
    for o_ref in (cmpT_ref, slcT_ref, winT_ref):
        hT = _nt_dot(wkv_ref[roff:roff + 256, :], xb)
        o_ref[0] = hT
        for g in range(NSA_KV):
            x1 = hT[g * HD: g * HD + ROT_HALF, :]
            x2 = hT[g * HD + ROT_HALF: g * HD + ROT_DIM, :]
            o_ref[0, g * HD: g * HD + ROT_HALF, :] = x1 * cosT - x2 * sinT
            o_ref[0, g * HD + ROT_HALF: g * HD + ROT_DIM, :] = x2 * cosT + x1 * sinT
        roff += 256
    for j in range(1024 // 256):
        fkvT_ref[0, j * 256:(j + 1) * 256, :] = _nt_dot(wkv_ref[roff + j * 256: roff + (j + 1) * 256, :], xb)
    roff += 1024
    flogT_ref[0] = _log_sigmoid(_nt_dot(wkv_ref[roff:roff + 8, :], xb) + bf_ref[...])


def _proj_prompt(x, g_pre, wq, wkv, tabs, b_fgate):
    b, t, d = x.shape
    tm = min(PROJ_TM, t)
    nt = t // tm
    c, shi, slo, cos, sin = tabs
    cosT, sinT = cos.T, sin.T
    row = lambda bi, ti: (bi, ti, 0)
    col = lambda bi, ti: (bi, 0, ti)
    const = lambda bi, ti: (0, 0)
    tab = lambda bi, ti: (ti, 0)
    tabT = lambda bi, ti: (0, ti)
    resident = dict(pipeline_mode=pl.Buffered(1))
    in_specs = [
        pl.BlockSpec((1, tm, d), row),
        pl.BlockSpec((1, d), const),
        pl.BlockSpec((NQ_ROWS, d), const, **resident),
        pl.BlockSpec((NKV_ROWS, d), const, **resident),
        pl.BlockSpec((tm, LANES), tab), pl.BlockSpec((tm, LANES), tab), pl.BlockSpec((tm, LANES), tab),
        pl.BlockSpec((ROT_HALF, tm), tabT), pl.BlockSpec((ROT_HALF, tm), tabT),
        pl.BlockSpec((FOX_H, 1), const),
    ]
    q_shape = jax.ShapeDtypeStruct((b, t, 512), BF16)
    out_shape = [q_shape] * 6 + [
        jax.ShapeDtypeStruct((b, t, 3072), BF16),
        jax.ShapeDtypeStruct((b, t, LANES), F32),
        jax.ShapeDtypeStruct((b, 256, t), F32), jax.ShapeDtypeStruct((b, 256, t), F32),
        jax.ShapeDtypeStruct((b, 256, t), F32),
        jax.ShapeDtypeStruct((b, 1024, t), F32),
        jax.ShapeDtypeStruct((b, 8, t), F32),
    ]
    out_specs = [pl.BlockSpec((1, tm, 512), row)] * 6 + [
        pl.BlockSpec((1, tm, 3072), row),
        pl.BlockSpec((1, tm, LANES), row),
        pl.BlockSpec((1, 256, tm), col), pl.BlockSpec((1, 256, tm), col), pl.BlockSpec((1, 256, tm), col),
        pl.BlockSpec((1, 1024, tm), col),
        pl.BlockSpec((1, 8, tm), col),
    ]
    return pl.pallas_call(
        _proj_prompt_kernel,
        grid=(b, nt),
        in_specs=in_specs,
        out_specs=out_specs,
        out_shape=out_shape,
        compiler_params=pltpu.CompilerParams(dimension_semantics=("arbitrary", "arbitrary"),
                                             vmem_limit_bytes=VMEM_LIMIT),
        name="proj_prompt",
    )(x, g_pre, wq, wkv, c, shi, slo, cosT, sinT, b_fgate.reshape(FOX_H, 1))


PROJ_S_TN = 1152


def _proj_sample_kernel(x_ref, g_ref, w_ref, c_ref, shi_ref, slo_ref, bf_ref, o_ref):
    xb = _rmsnorm_rows(x_ref[...], g_ref[...]).astype(BF16)
    o_ref[...] = _nt_dot(xb, w_ref[...])

    @pl.when(pl.program_id(0) == _S_OFF["flog"] // PROJ_S_TN)
    def _():
        lo = _S_OFF["flog"] % PROJ_S_TN
        o_ref[:, lo:lo + FOX_H] = _log_sigmoid(o_ref[:, lo:lo + FOX_H] + bf_ref[...])

    @pl.when(pl.program_id(0) == 0)
    def _():
        c, shi, slo = c_ref[...], shi_ref[...], slo_ref[...]
        for c0 in (0, 128, 256, 384, 512, 768, 1024):
            o_ref[:, c0:c0 + LANES] = _rope_lanes(o_ref[:, c0:c0 + LANES], c, shi, slo)


def _proj_sample(x, g_pre, w_all, tabs, b_fgate):
    n, d = x.shape
    c, shi, slo, _, _ = tabs
    const = lambda j: (0, 0)
    return pl.pallas_call(
        _proj_sample_kernel,
        grid=(N_IN_PAD // PROJ_S_TN,),
        in_specs=[pl.BlockSpec((n, d), const), pl.BlockSpec((1, d), const),
                  pl.BlockSpec((PROJ_S_TN, d), lambda j: (j, 0)),
                  pl.BlockSpec((n, LANES), const), pl.BlockSpec((n, LANES), const), pl.BlockSpec((n, LANES), const),
                  pl.BlockSpec((1, FOX_H), const)],
        out_specs=pl.BlockSpec((n, PROJ_S_TN), lambda j: (0, j)),
        out_shape=jax.ShapeDtypeStruct((n, N_IN_PAD), F32),
        compiler_params=pltpu.CompilerParams(dimension_semantics=("arbitrary",)),
        name="proj_sample",
    )(x, g_pre, w_all, c, shi, slo, b_fgate.reshape(1, FOX_H))


CMP_PITCH = 72
CMP_DCHUNK = 8


def _compress_kernel(pt_ref, src_ref, w1_ref, w2_ref, pe_ref, o_ref, buf, sem, *, n_pages, paged):
    bi, kv = pl.program_id(0), pl.program_id(1)
    n_steps = pl.num_programs(0) * 2
    step = bi * 2 + kv
    slot = step % 2

    def page_copy(b_, kv_, i, slot_):
        if paged:
            src = src_ref.at[pt_ref[b_, i], kv_]
        else:
            src = src_ref.at[b_, kv_, :, :, pl.ds(i * PAGE_SIZE, PAGE_SIZE)]
        dst = buf.at[slot_, :, pl.ds(i * CMP_PITCH, HD), :]
        return pltpu.make_async_copy(src, dst, sem.at[slot_])

    def issue(b_, kv_, slot_):
        for i in range(n_pages):
            page_copy(b_, kv_, i, slot_).start()

    @pl.when(step == 0)
    def _():
        issue(bi, kv, slot)

    @pl.when(step + 1 < n_steps)
    def _():
        nxt = step + 1
        issue(nxt // 2, nxt % 2, 1 - slot)

    for i in range(n_pages):
        page_copy(0, 0, i, slot).wait()

    acc = jnp.zeros((2 * n_pages, LANES), F32)
    for c in range(HD // CMP_DCHUNK):
        cols = []
        for dd in range(CMP_DCHUNK):
            d = c * CMP_DCHUNK + dd
            rows = [buf[slot, g, pl.ds(d, n_pages, stride=CMP_PITCH), :] for g in range(NSA_KV)]
            x = jnp.concatenate(rows, axis=0) + pe_ref[0, d:d + 1, :]
            cols.append(x.astype(BF16))
        lhs = jnp.concatenate(cols, axis=1)
        k0 = c * CMP_DCHUNK * LANES
        acc = acc + _nn_dot(lhs, w1_ref[0, k0:k0 + CMP_DCHUNK * LANES, :])
    hid = _silu(acc)
    out = _nn_dot(hid.astype(BF16), w2_ref[0])
    o_ref[0, 0, 0] = out[:n_pages]
    o_ref[0, 0, 1] = out[n_pages:]


def _compress(src, page_table, w1bd, w2bd, pe_t, *, n_batch, n_pages, paged):
    kern = functools.partial(_compress_kernel, n_pages=n_pages, paged=paged)
    grid_spec = pltpu.PrefetchScalarGridSpec(
        num_scalar_prefetch=1,
        grid=(n_batch, 2),
        in_specs=[pl.BlockSpec(memory_space=pl.ANY),
                  pl.BlockSpec((1, HD * LANES, LANES), lambda b, kv, pt: (kv, 0, 0)),
                  pl.BlockSpec((1, LANES, LANES), lambda b, kv, pt: (kv, 0, 0)),
                  pl.BlockSpec((1, HD, LANES), lambda b, kv, pt: (kv, 0, 0))],
        out_specs=pl.BlockSpec((1, 1, NSA_KV, n_pages, LANES), lambda b, kv, pt: (b, kv, 0, 0, 0)),
        scratch_shapes=[pltpu.VMEM((2, NSA_KV, n_pages * CMP_PITCH, LANES), F32),
                        pltpu.SemaphoreType.DMA((2,))],
    )
    return pl.pallas_call(
        kern, grid_spec=grid_spec,
        out_shape=jax.ShapeDtypeStruct((n_batch, 2, NSA_KV, n_pages, LANES), F32),
        compiler_params=pltpu.CompilerParams(dimension_semantics=("arbitrary", "arbitrary"),
                                             vmem_limit_bytes=VMEM_LIMIT),
        name="compress_paged" if paged else "compress_prompt",
    )(page_table, src, w1bd, w2bd, pe_t)


def _compress_weights(w1, w2, pe):
    eye2 = jnp.eye(2, dtype=w1.dtype)
    w1bd = jnp.einsum('klde,nm->kdnlme', w1, eye2).reshape(2, HD * LANES, LANES).astype(BF16)
    w2bd = jnp.einsum('kef,nm->knemf', w2, eye2).reshape(2, LANES, LANES).astype(BF16)
    pe_t = jnp.tile(jnp.transpose(pe, (0, 2, 1)), (1, 1, 2))
    return w1bd, w2bd, pe_t


def _log_sigmoid(x):
    return jnp.minimum(x, 0.0) - jnp.log1p(jnp.exp(-jnp.abs(x)))


def _lane_cumsum(x):
    lane = lax.broadcasted_iota(jnp.int32, x.shape, 1)
    s = 1
    while s < LANES:
        x = x + jnp.where(lane >= s, pltpu.roll(x, s, axis=1), 0.0)
        s *= 2
    return x


def _online_update(carry, s, v_t, mask):
    m, l, acc = carry
    if mask is not None:
        s = jnp.where(mask, s, NEG)
    m_new = jnp.maximum(m, jnp.max(s, axis=1, keepdims=True))
    alpha = jnp.exp(m - m_new)
    p = jnp.exp(s - m_new)
    if mask is not None:
        p = jnp.where(mask, p, 0.0)
    l = l * alpha + jnp.sum(p, axis=1, keepdims=True)
    acc = acc * alpha + _nt_dot(p.astype(BF16), v_t)
    return m_new, l, acc


def _online_init(rows, d):
    return (jnp.full((rows, 1), NEG, F32), jnp.zeros((rows, 1), F32), jnp.zeros((rows, d), F32))


def _online_finish(carry):
    _, l, acc = carry
    return acc / jnp.maximum(l, TINY)


ATT_T = 256


def _fox_prompt_kernel(q_ref, k_ref, v_ref, logf_ref, gate_ref, o_ref, k_t, v_t, c_scr):
    hp, qi = pl.program_id(1), pl.program_id(2)
    tk = ATT_T
    t = k_ref.shape[2]
    nkt = t // tk

    @pl.when(qi == 0)
    def _():
        for j in range(2):
            for kt in range(nkt):
                k_t[j, kt] = k_ref[0, j * HD:(j + 1) * HD, kt * tk:(kt + 1) * tk].astype(BF16)
                v_t[j, kt] = v_ref[0, j * HD:(j + 1) * HD, kt * tk:(kt + 1) * tk].astype(BF16)
        carry = jnp.zeros((FOX_H, 1), F32)
        per = tk // LANES
        for ch in range(t // LANES):
            lf = logf_ref[0, :, ch * LANES:(ch + 1) * LANES]
            x = _lane_cumsum(lf) + carry
            carry = x[:, LANES - 1:LANES]
            head = lax.broadcasted_iota(jnp.int32, x.shape, 0)
            for j in range(2):
                xr = jnp.sum(jnp.where(head == 2 * hp + j, x, 0.0), axis=0, keepdims=True)
                c_scr[j, ch // per, :, (ch % per) * LANES:(ch % per + 1) * LANES] = xr

    row = lax.broadcasted_iota(jnp.int32, (tk, tk), 0)
    col = lax.broadcasted_iota(jnp.int32, (tk, tk), 1)
    causal = col <= row
    outs = []
    for j in range(2):
        q = q_ref[0, :, j * HD:(j + 1) * HD]
        c_q = c_scr[j, qi][:, 0:1]

        def logits(kt, q=q, c_q=c_q, j=j):
            return _nn_dot(q, k_t[j, kt]) + (c_q - c_scr[j, kt])

        def body(kt, carry, j=j, logits=logits):
            return _online_update(carry, logits(kt), v_t[j, kt], None)

        carry = lax.fori_loop(0, qi, body, _online_init(tk, HD))
        carry = _online_update(carry, logits(qi), v_t[j, qi], causal)
        outs.append(_online_finish(carry))
    o = jnp.concatenate(outs, axis=1) * _silu(gate_ref[0].astype(F32))
    o_ref[0] = o.astype(o_ref.dtype)


def _fox_prompt(fq, fkv_t, logf_t, fsilu):
    b, t, _ = fq.shape
    tq = ATT_T
    nkt = t // tq
    return pl.pallas_call(
        _fox_prompt_kernel,
        grid=(b, FOX_H // 2, t // tq),
        in_specs=[pl.BlockSpec((1, tq, LANES), lambda bi, hp, qi: (bi, qi, hp)),
                  pl.BlockSpec((1, LANES, t), lambda bi, hp, qi: (bi, hp, 0)),
                  pl.BlockSpec((1, LANES, t), lambda bi, hp, qi: (bi, FOX_H // 2 + hp, 0)),
                  pl.BlockSpec((1, FOX_H, t), lambda bi, hp, qi: (bi, 0, 0)),
                  pl.BlockSpec((1, tq, LANES), lambda bi, hp, qi: (bi, qi, hp))],
        out_specs=pl.BlockSpec((1, tq, LANES), lambda bi, hp, qi: (bi, qi, hp)),
        out_shape=jax.ShapeDtypeStruct((b, t, BR_W), BF16),
        scratch_shapes=[pltpu.VMEM((2, nkt, HD, tq), BF16), pltpu.VMEM((2, nkt, HD, tq), BF16),
                        pltpu.VMEM((2, nkt, 1, tq), F32)],
        compiler_params=pltpu.CompilerParams(dimension_semantics=("arbitrary",) * 3, vmem_limit_bytes=VMEM_LIMIT),
        name="fox_prompt",
    )(fq, fkv_t, fkv_t, logf_t, fsilu)


def _nsa_prompt_kernel(q_ref, ks_ref, vs_ref, kw_ref, vw_ref, kc_ref, vc_ref, ngate_ref, gate_ref, o_ref,
                       ks_t, vs_t, kw_t, vw_t):
    g, qi = pl.program_id(1), pl.program_id(2)
    tk = ATT_T
    tq = ATT_T
    t = ks_ref.shape[2]
    nkt = t // tk
    nb = t // CMP_BLOCK
    rows = NSA_R * tq

    @pl.when(qi == 0)
    def _():
        for kt in range(nkt):
            sl = slice(kt * tk, (kt + 1) * tk)
            ks_t[kt] = ks_ref[0, :, sl].astype(BF16)
            vs_t[kt] = vs_ref[0, :, sl].astype(BF16)
            kw_t[kt] = kw_ref[0, :, sl].astype(BF16)
            vw_t[kt] = vw_ref[0, :, sl].astype(BF16)

    q4 = jnp.concatenate([q_ref[0, :, r * HD:(r + 1) * HD] for r in range(NSA_R)], axis=0)
    q0 = qi * tq
    tpos4 = q0 + lax.broadcasted_iota(jnp.int32, (rows, nb), 0) % tq
    blk4 = lax.broadcasted_iota(jnp.int32, (rows, nb), 1)
    readable = (blk4 + 1) * CMP_BLOCK - 1 <= tpos4
    s_c = jnp.where(readable, _nt_dot(q4, kc_ref[0, 0, 0].astype(BF16)), NEG)
    m_c = jnp.max(s_c, axis=1, keepdims=True)
    e_c = jnp.where(readable, jnp.exp(s_c - m_c), 0.0)
    p_c = e_c / jnp.maximum(jnp.sum(e_c, axis=1, keepdims=True), TINY)
    o_c = _nn_dot(p_c.astype(BF16), vc_ref[0, 0, 0].astype(BF16))
    imp = p_c[0:tq] + p_c[tq:2 * tq] + p_c[2 * tq:3 * tq] + p_c[3 * tq:4 * tq]
    tpos = q0 + lax.broadcasted_iota(jnp.int32, (tq, nb), 0)
    blk = lax.broadcasted_iota(jnp.int32, (tq, nb), 1)
    cur = tpos // CMP_BLOCK
    score = jnp.where(blk < cur, imp, jnp.where(blk == cur, BIG, NEG))
    rank = jnp.zeros((tq, nb), F32)
    for mth in range(nb):
        colv = score[:, mth:mth + 1]
        ahead = (colv > score) | ((colv == score) & (blk > mth))
        rank = rank + jnp.where(ahead, 1.0, 0.0)
    sel = jnp.where((rank < N_SELECT) & (score > 0.5 * NEG), 1.0, 0.0).astype(BF16)

    row = lax.broadcasted_iota(jnp.int32, (rows, tk), 0) % tq
    col = lax.broadcasted_iota(jnp.int32, (rows, tk), 1)
    causal = col <= row

    def sel_mask(kt):
        kb = (kt * tk + lax.broadcasted_iota(jnp.int32, (nb, tk), 1)) // CMP_BLOCK
        expand = jnp.where(kb == lax.broadcasted_iota(jnp.int32, (nb, tk), 0), 1.0, 0.0).astype(BF16)
        mk = _nn_dot(sel, expand) > 0.5
        return jnp.concatenate([mk] * NSA_R, axis=0)

    def sel_body(kt, carry):
        return _online_update(carry, _nn_dot(q4, ks_t[kt]), vs_t[kt], sel_mask(kt))

    carry = lax.fori_loop(0, qi, sel_body, _online_init(rows, HD))
    carry = _online_update(carry, _nn_dot(q4, ks_t[qi]), vs_t[qi], sel_mask(qi) & causal)
    o_s = _online_finish(carry)
    carry = _online_init(rows, HD)
    k2 = jnp.maximum(qi - 2, 0)
    carry = _online_update(carry, _nn_dot(q4, kw_t[k2]), vw_t[k2], (col >= row) & (qi >= 2))
    k1 = jnp.maximum(qi - 1, 0)
    carry = _online_update(carry, _nn_dot(q4, kw_t[k1]), vw_t[k1], (col >= 0) & (qi >= 1))
    carry = _online_update(carry, _nn_dot(q4, kw_t[qi]), vw_t[qi], causal)
    o_w = _online_finish(carry)
    gts = _sigmoid(ngate_ref[0])
    def gate_col(br, r):
        c0 = br * NSA_H + r
        return jnp.where(g == 0, gts[:, c0:c0 + 1], gts[:, c0 + NSA_R:c0 + NSA_R + 1])

    outs = []
    for r in range(NSA_R):
        sl = slice(r * tq, (r + 1) * tq)
        outs.append(gate_col(0, r) * o_c[sl] + gate_col(1, r) * o_s[sl] + gate_col(2, r) * o_w[sl])
    o = jnp.concatenate(outs, axis=1) * _silu(gate_ref[0].astype(F32))
    o_ref[0] = o.astype(o_ref.dtype)


def _nsa_prompt(nq, slc_t, win_t, kcb, vcb, ngate, nsilu):
    b, t, _ = nq.shape
    tq = ATT_T
    nkt = t // tq
    nb = t // CMP_BLOCK
    gw = NSA_R * HD
    kv_spec = lambda kv: pl.BlockSpec((1, HD, t), lambda bi, g, qi, kv=kv: (bi, kv * NSA_KV + g, 0))
    cb_spec = lambda kv: pl.BlockSpec((1, 1, 1, nb, HD), lambda bi, g, qi, kv=kv: (bi, kv, g, 0, 0))
    return pl.pallas_call(
        _nsa_prompt_kernel,
        grid=(b, NSA_KV, t // tq),
        in_specs=[pl.BlockSpec((1, tq, gw), lambda bi, g, qi: (bi, qi, g)),
                  kv_spec(0), kv_spec(1), kv_spec(0), kv_spec(1),
                  cb_spec(0), cb_spec(1),
                  pl.BlockSpec((1, tq, LANES), lambda bi, g, qi: (bi, qi, 0)),
                  pl.BlockSpec((1, tq, gw), lambda bi, g, qi: (bi, qi, g))],
        out_specs=pl.BlockSpec((1, tq, gw), lambda bi, g, qi: (bi, qi, g)),
        out_shape=jax.ShapeDtypeStruct((b, t, BR_W), BF16),
        scratch_shapes=[pltpu.VMEM((nkt, HD, tq), BF16)] * 4,
        compiler_params=pltpu.CompilerParams(dimension_semantics=("arbitrary",) * 3, vmem_limit_bytes=VMEM_LIMIT),
        name="nsa_prompt",
    )(nq, slc_t, slc_t, win_t, win_t, kcb, vcb, ngate, nsilu)


def _mem_kv_kernel(m_ref, g_ref, w_ref, o_ref):
    o_ref[0] = _nn_dot(_rmsnorm_rows(m_ref[0], g_ref[...]).astype(BF16), w_ref[...])


def _mem_kv(mem, g_mem, w_bf):
    b, m, d = mem.shape
    n = w_bf.shape[1]
    return pl.pallas_call(
        _mem_kv_kernel, grid=(b,),
        in_specs=[pl.BlockSpec((1, m, d), lambda bi: (bi, 0, 0)), pl.BlockSpec((1, d), lambda bi: (0, 0)),
                  pl.BlockSpec((d, n), lambda bi: (0, 0))],
        out_specs=pl.BlockSpec((1, m, n), lambda bi: (bi, 0, 0)),
        out_shape=jax.ShapeDtypeStruct((b, m, n), F32),
        compiler_params=pltpu.CompilerParams(dimension_semantics=("arbitrary",)),
        name="mem_kv",
    )(mem, g_mem, w_bf)


MEM_SCALE = MEM_HD ** -0.5


def _mem_attn_kernel(mq_ref, ms_ref, mkv_ref, o_ref):
    mq = mq_ref[0].astype(BF16)
    outs = []
    for h in range(MEM_H):
        kh = mkv_ref[0, :, h * MEM_HD:(h + 1) * MEM_HD].astype(BF16)
        vh = mkv_ref[0, :, BR_W + h * MEM_HD: BR_W + (h + 1) * MEM_HD].astype(BF16)
        s = _nt_dot(mq[:, h * MEM_HD:(h + 1) * MEM_HD], kh) * MEM_SCALE
        e = jnp.exp(s - jnp.max(s, axis=1, keepdims=True))
        p = e / jnp.sum(e, axis=1, keepdims=True)
        outs.append(_nn_dot(p.astype(BF16), vh))
    o_ref[0] = (jnp.concatenate(outs, axis=1) * _silu(ms_ref[0].astype(F32))).astype(o_ref.dtype)


MERGE_TM = 512


def _mem_attn(mq, msilu, mkv):
    b, t, _ = mq.shape
    tm = min(MERGE_TM, t)
    m = mkv.shape[1]
    row = lambda bi, ti: (bi, ti, 0)
    return pl.pallas_call(
        _mem_attn_kernel, grid=(b, t // tm),
        in_specs=[pl.BlockSpec((1, tm, BR_W), row), pl.BlockSpec((1, tm, BR_W), row),
                  pl.BlockSpec((1, m, 2 * BR_W), lambda bi, ti: (bi, 0, 0))],
        out_specs=pl.BlockSpec((1, tm, BR_W), row),
        out_shape=jax.ShapeDtypeStruct((b, t, BR_W), BF16),
        compiler_params=pltpu.CompilerParams(dimension_semantics=("arbitrary", "arbitrary")),
        name="mem_attn",
    )(mq, msilu, mkv)


def _merge_kernel(an_ref, af_ref, am_ref, bg_ref, x_ref, wb_ref, wo_ref, gp_ref, o_ref):
    d = wo_ref.shape[0]
    merged = None
    for k, a_ref in enumerate((an_ref, af_ref, am_ref)):
        z = _nn_dot(a_ref[0].astype(BF16), wb_ref[k])
        term = _sigmoid(bg_ref[0, :, k * d:(k + 1) * d].astype(F32)) * z
        merged = term if merged is None else merged + term
    y = _nn_dot(merged.astype(BF16), wo_ref[...])
    o_ref[0] = x_ref[0] + _rmsnorm_rows(y, gp_ref[...])


def _merge(a_nsa, a_fox, a_mem, bgate, x, wb, wo, g_post):
    b, t, d = x.shape
    tm = min(MERGE_TM, t)
    row = lambda bi, ti: (bi, ti, 0)
    resident = dict(pipeline_mode=pl.Buffered(1))
    return pl.pallas_call(
        _merge_kernel, grid=(b, t // tm),
        in_specs=[pl.BlockSpec((1, tm, BR_W), row), pl.BlockSpec((1, tm, BR_W), row), pl.BlockSpec((1, tm, BR_W), row),
                  pl.BlockSpec((1, tm, 3 * d), row), pl.BlockSpec((1, tm, d), row),
                  pl.BlockSpec((3, BR_W, d), lambda bi, ti: (0, 0, 0), **resident),
                  pl.BlockSpec((d, d), lambda bi, ti: (0, 0), **resident),
                  pl.BlockSpec((1, d), lambda bi, ti: (0, 0))],
        out_specs=pl.BlockSpec((1, tm, d), row),
        out_shape=jax.ShapeDtypeStruct((b, t, d), F32),
        compiler_params=pltpu.CompilerParams(dimension_semantics=("arbitrary", "arbitrary"),
                                             vmem_limit_bytes=VMEM_LIMIT),
        name="merge",
    )(a_nsa, a_fox, a_mem, bgate, x, wb, wo, g_post)


_S_ORDER = ("nq", "ncmp", "nslc", "nwin", "nsilu", "fq", "fkv", "fsilu", "mq", "msilu", "bgate", "ngate", "flog")
_S_OFF = {}
_o = 0
for _k in _S_ORDER:
    _S_OFF[_k] = _o
    _o += _COLS[_k][1] - _COLS[_k][0]
assert _o == N_IN and _S_OFF["ngate"] % LANES == 0
N_PAST_SEL = N_SELECT - 1


def _row_to_col(row):
    n = row.shape[1]
    eye = lax.broadcasted_iota(jnp.int32, (n, n), 0) == lax.broadcasted_iota(jnp.int32, (n, n), 1)
    return jnp.sum(jnp.where(eye, row, 0.0), axis=1, keepdims=True)


def _col_to_row(col):
    n = col.shape[0]
    eye = lax.broadcasted_iota(jnp.int32, (n, n), 0) == lax.broadcasted_iota(jnp.int32, (n, n), 1)
    return jnp.sum(jnp.where(eye, col, 0.0), axis=0, keepdims=True)


def _hrow(hs_ref, key, off=0, n=None):
    lo = _S_OFF[key] + off
    n = (_COLS[key][1] - _COLS[key][0] - off) if n is None else n
    return hs_ref[0, :, lo:lo + n]


def _q_group(hs_ref, key, g):
    return jnp.concatenate([_hrow(hs_ref, key, (g * NSA_R + r) * HD, HD) for r in range(NSA_R)], axis=0) * SCALE


def _nsa_sample_sel_kernel(hs_ref, cb_ref, oc_ref, idx_ref):
    nblk = cb_ref.shape[3]
    idx_ref[...] = jnp.zeros(idx_ref.shape, idx_ref.dtype)
    for g in range(NSA_KV):
        q4 = _q_group(hs_ref, "nq", g).astype(BF16)
        s = _nt_dot(q4, cb_ref[0, 0, g].astype(BF16))
        e = jnp.exp(s - jnp.max(s, axis=1, keepdims=True))
        p = e / jnp.sum(e, axis=1, keepdims=True)
        oc_ref[0, g * NSA_R:(g + 1) * NSA_R, :] = _nn_dot(p.astype(BF16), cb_ref[0, 1, g].astype(BF16))
        imp = jnp.sum(p, axis=0, keepdims=True)
        col = _row_to_col(imp)
        mi = lax.broadcasted_iota(jnp.int32, (nblk, nblk), 0)
        ni = lax.broadcasted_iota(jnp.int32, (nblk, nblk), 1)
        ahead = (col > imp) | ((col == imp) & (mi < ni))
        rank = jnp.sum(jnp.where(ahead, 1.0, 0.0), axis=0, keepdims=True)
        sel = jnp.where(rank < N_PAST_SEL, 1.0, 0.0)
        pos, carry = [], jnp.zeros((1, 1), F32)
        for ch in range(nblk // LANES):
            chunk = sel[:, ch * LANES:(ch + 1) * LANES]
            incl = _lane_cumsum(chunk) + carry
            carry = incl[:, LANES - 1:LANES]
            pos.append(incl - chunk)
        pos = jnp.concatenate(pos, axis=1)
        slot = lax.broadcasted_iota(jnp.int32, (N_SELECT, nblk), 0).astype(F32)
        onehot = jnp.where((pos == slot) & (sel > 0.5), 1.0, 0.0).astype(BF16)
        ids = lax.broadcasted_iota(jnp.int32, (1, nblk), 1).astype(F32).astype(BF16)
        idx = _nt_dot(ids, onehot)
        idx_ref[0, g:g + 1, 0:N_SELECT] = idx.astype(jnp.int32)


def _nsa_sample_sel(hs3, cb):
    n = hs3.shape[0]
    nblk = cb.shape[3]
    assert nblk % LANES == 0 and nblk >= N_PAST_SEL and nblk <= 256
    return pl.pallas_call(
        _nsa_sample_sel_kernel, grid=(n,),
        in_specs=[pl.BlockSpec((1, 1, N_IN_PAD), lambda b: (b, 0, 0)),
                  pl.BlockSpec((1, 2, NSA_KV, nblk, HD), lambda b: (b, 0, 0, 0, 0))],
        out_specs=[pl.BlockSpec((1, NSA_H, HD), lambda b: (b, 0, 0)),
                   pl.BlockSpec((1, NSA_KV, LANES), lambda b: (b, 0, 0))],
        out_shape=[jax.ShapeDtypeStruct((n, NSA_H, HD), F32), jax.ShapeDtypeStruct((n, NSA_KV, LANES), jnp.int32)],
        compiler_params=pltpu.CompilerParams(dimension_semantics=("arbitrary",)),
        name="nsa_sample_select",
    )(hs3, cb)


def _nsa_sample_attn_kernel(idx_ref, pt_ref, hs_ref, oc_ref, win_ref, *refs):
    n_blk = NSA_KV * N_PAST_SEL
    blk_refs, (o_ref, nw_ref) = refs[:n_blk], refs[n_blk:]
    b = pl.program_id(0)
    gates = _sigmoid(_hrow(hs_ref, "ngate"))
    lane_half = lax.broadcasted_iota(jnp.int32, (NSA_R, PAGE_SIZE), 1) // CMP_BLOCK
    outs = []
    for g in range(NSA_KV):
        q4 = _q_group(hs_ref, "nq", g)
        q4b = q4.astype(BF16)
        kn, vn = _hrow(hs_ref, "nslc", g * HD, HD), _hrow(hs_ref, "nslc", KV_W + g * HD, HD)
        s_cur = jnp.sum(q4 * kn, axis=1, keepdims=True)
        s_list, masks = [], []
        for k in range(N_PAST_SEL):
            half = idx_ref[b, g * LANES + k] & 1
            mask = lane_half == half
            s = _nn_dot(q4b, blk_refs[g * N_PAST_SEL + k][0, 0, 0].astype(BF16))
            s_list.append(jnp.where(mask, s, NEG))
            masks.append(mask)
        m = s_cur
        for s in s_list:
            m = jnp.maximum(m, jnp.max(s, axis=1, keepdims=True))
        l = jnp.exp(s_cur - m)
        acc = l * vn
        for k in range(N_PAST_SEL):
            e = jnp.where(masks[k], jnp.exp(s_list[k] - m), 0.0)
            l = l + jnp.sum(e, axis=1, keepdims=True)
            acc = acc + _nt_dot(e.astype(BF16), blk_refs[g * N_PAST_SEL + k][0, 1, 0].astype(BF16))
        o_s = acc / jnp.maximum(l, TINY)
        kwn, vwn = _hrow(hs_ref, "nwin", g * HD, HD), _hrow(hs_ref, "nwin", KV_W + g * HD, HD)
        s_w = _nn_dot(q4b, win_ref[0, 0, g].astype(BF16))
        s_wn = jnp.sum(q4 * kwn, axis=1, keepdims=True)
        m = jnp.maximum(jnp.max(s_w, axis=1, keepdims=True), s_wn)
        e, en = jnp.exp(s_w - m), jnp.exp(s_wn - m)
        l = jnp.sum(e, axis=1, keepdims=True) + en
        o_w = (_nt_dot(e.astype(BF16), win_ref[0, 1, g].astype(BF16)) + en * vwn) / jnp.maximum(l, TINY)
        o_c = oc_ref[0, g * NSA_R:(g + 1) * NSA_R, :]
        for r in range(NSA_R):
            c = g * NSA_R + r
            gate = lambda br: gates[:, br * NSA_H + c: br * NSA_H + c + 1]
            outs.append(gate(0) * o_c[r:r + 1] + gate(1) * o_s[r:r + 1] + gate(2) * o_w[r:r + 1])
    o_ref[0] = jnp.concatenate(outs, axis=1) * _silu(_hrow(hs_ref, "nsilu"))
    wb = win_ref.shape[4]
    newcol = _row_to_col(_hrow(hs_ref, "nwin"))
    last = lax.broadcasted_iota(jnp.int32, (HD, wb), 1) == wb - 1
    for kv in range(2):
        for g in range(NSA_KV):
            r0 = (kv * NSA_KV + g) * HD
            shifted = pltpu.roll(win_ref[0, kv, g], wb - 1, axis=1)
            nw_ref[0, kv, g] = jnp.where(last, newcol[r0:r0 + HD], shifted)


def _nsa_sample_attn(idx, page_table, hs3, o_c, win_state, pool_slc):
    n = hs3.shape[0]
    wb = win_state.shape[4]
    assert wb == WINDOW
    blk_specs = [pl.BlockSpec((1, 2, 1, HD, PAGE_SIZE),
                              lambda b, idx, pt, g=g, k=k: (pt[b, idx[b, g * LANES + k] >> 1], 0, g, 0, 0))
                 for g in range(NSA_KV) for k in range(N_PAST_SEL)]
    grid_spec = pltpu.PrefetchScalarGridSpec(
        num_scalar_prefetch=2, grid=(n,),
        in_specs=[pl.BlockSpec((1, 1, N_IN_PAD), lambda b, idx, pt: (b, 0, 0)),
                  pl.BlockSpec((1, NSA_H, HD), lambda b, idx, pt: (b, 0, 0)),
                  pl.BlockSpec((1, 2, NSA_KV, HD, wb), lambda b, idx, pt: (b, 0, 0, 0, 0))] + blk_specs,
        out_specs=[pl.BlockSpec((1, 1, BR_W), lambda b, idx, pt: (b, 0, 0)),
                   pl.BlockSpec((1, 2, NSA_KV, HD, wb), lambda b, idx, pt: (b, 0, 0, 0, 0))],
    )
    return pl.pallas_call(
        _nsa_sample_attn_kernel, grid_spec=grid_spec,
        out_shape=[jax.ShapeDtypeStruct((n, 1, BR_W), F32), jax.ShapeDtypeStruct((n, 2, NSA_KV, HD, wb), F32)],
        compiler_params=pltpu.CompilerParams(dimension_semantics=("arbitrary",)),
        name="nsa_sample_attn",
    )(idx, page_table, hs3, o_c, win_state, *([pool_slc] * (NSA_KV * N_PAST_SEL)))


FOX_NPG = 8


def _fox_sample_kernel(pt_ref, hs_ref, *refs):
    kv_refs, lf_refs = refs[:FOX_NPG], refs[FOX_NPG:2 * FOX_NPG]
    o_ref, q_b, acc, m_s, l_s, c_s = refs[2 * FOX_NPG:]
    j = pl.program_id(1)

    @pl.when(j == 0)
    def _():
        for h in range(FOX_H):
            q_b[h] = jnp.broadcast_to(_row_to_col(_hrow(hs_ref, "fq", h * HD, HD) * SCALE), (HD, PAGE_SIZE))
        acc[...] = jnp.zeros(acc.shape, F32)
        m_s[...] = jnp.full(m_s.shape, NEG, F32)
        l_s[...] = jnp.zeros(l_s.shape, F32)
        c_s[...] = jnp.zeros(c_s.shape, F32)

    a_new = _row_to_col(_hrow(hs_ref, "flog"))
    c = c_s[...]
    s_list = [None] * FOX_NPG
    for i in reversed(range(FOX_NPG)):
        incl = _lane_cumsum(lf_refs[i][0])
        total = incl[:, PAGE_SIZE - 1:PAGE_SIZE]
        r_past = c + (total - incl)
        c = c + total
        qk = jnp.concatenate([jnp.sum(kv_refs[i][0, 0, h] * q_b[h], axis=0, keepdims=True) for h in range(FOX_H)], axis=0)
        s_list[i] = qk + a_new + r_past
    c_s[...] = c
    m_old = m_s[...]
    m_new = m_old
    for s in s_list:
        m_new = jnp.maximum(m_new, jnp.max(s, axis=1, keepdims=True))
    alpha = jnp.exp(m_old - m_new)
    p_list = [jnp.exp(s - m_new) for s in s_list]
    l_new = l_s[...] * alpha
    for p in p_list:
        l_new = l_new + jnp.sum(p, axis=1, keepdims=True)
    m_s[...] = m_new
    l_s[...] = l_new
    for h in range(FOX_H):
        a = acc[h] * alpha[h:h + 1, :]
        for i in range(FOX_NPG):
            a = a + kv_refs[i][0, 1, h] * p_list[i][h:h + 1, :]
        acc[h] = a

    @pl.when(j == pl.num_programs(1) - 1)
    def _():
        outs = []
        for h in range(FOX_H):
            q = _hrow(hs_ref, "fq", h * HD, HD) * SCALE
            kn, vn = _hrow(hs_ref, "fkv", h * HD, HD), _hrow(hs_ref, "fkv", FOX_H * HD + h * HD, HD)
            s_n = jnp.sum(q * kn, axis=1, keepdims=True)
            m_h, l_h = m_s[h:h + 1, :], l_s[h:h + 1, :]
            m_f = jnp.maximum(m_h, s_n)
            al, pn = jnp.exp(m_h - m_f), jnp.exp(s_n - m_f)
            past = _col_to_row(jnp.sum(acc[h], axis=1, keepdims=True))
            outs.append((past * al + pn * vn) / jnp.maximum(l_h * al + pn, TINY))
        o_ref[0] = jnp.concatenate(outs, axis=1) * _silu(_hrow(hs_ref, "fsilu"))


def _fox_sample(page_table, hs3, pool_kv, pool_lf):
    n, n_pages = page_table.shape
    assert n_pages % FOX_NPG == 0
    ng = n_pages // FOX_NPG
    page = lambda i: (lambda b, j, pt: pt[b, (ng - 1 - j) * FOX_NPG + i])
    kv_specs = [pl.BlockSpec((1, 2, FOX_H, HD, PAGE_SIZE), lambda b, j, pt, f=page(i): (f(b, j, pt), 0, 0, 0, 0))
                for i in range(FOX_NPG)]
    lf_specs = [pl.BlockSpec((1, FOX_H, PAGE_SIZE), lambda b, j, pt, f=page(i): (f(b, j, pt), 0, 0))
                for i in range(FOX_NPG)]
    grid_spec = pltpu.PrefetchScalarGridSpec(
        num_scalar_prefetch=1, grid=(n, ng),
        in_specs=[pl.BlockSpec((1, 1, N_IN_PAD), lambda b, j, pt: (b, 0, 0)),
                  ] + kv_specs + lf_specs,
        out_specs=pl.BlockSpec((1, 1, BR_W), lambda b, j, pt: (b, 0, 0)),
        scratch_shapes=[pltpu.VMEM((FOX_H, HD, PAGE_SIZE), F32), pltpu.VMEM((FOX_H, HD, PAGE_SIZE), F32),
                        pltpu.VMEM((FOX_H, 1), F32), pltpu.VMEM((FOX_H, 1), F32), pltpu.VMEM((FOX_H, 1), F32)],
    )
    return pl.pallas_call(
        _fox_sample_kernel, grid_spec=grid_spec,
        out_shape=jax.ShapeDtypeStruct((n, 1, BR_W), F32),
        compiler_params=pltpu.CompilerParams(dimension_semantics=("arbitrary", "arbitrary"),
                                             vmem_limit_bytes=VMEM_LIMIT),
        name="fox_sample",
    )(page_table, hs3, *([pool_kv] * FOX_NPG), *([pool_lf] * FOX_NPG))


def _split_weights(w_in):
    wt = w_in.T
    sl = lambda k: wt[_COLS[k][0]:_COLS[k][1]]
    d = wt.shape[1]
    wq = jnp.concatenate([sl(k) for k in _QGROUPS] + [sl("bgate"), sl("ngate"),
                                                        jnp.zeros((LANES - 24, d), wt.dtype)], axis=0).astype(BF16)
    wkv = jnp.concatenate([sl(k) for k, _ in _KVGROUPS], axis=0).astype(BF16)
    w_all = jnp.concatenate([sl(k) for k in _S_ORDER] + [jnp.zeros((N_IN_PAD - N_IN, d), wt.dtype)],
                            axis=0).astype(BF16)
    return wq, wkv, w_all


def kernel(x_prompt, x_sample, cache_nsa_cmp, cache_nsa_slc, state_nsa_win, cache_fox_kv, cache_fox_logf, cache_mem_kv, page_table, mem_prompt, g_pre, g_post, g_mem, w_in, b_fgate, w_mem_kv, w_cmp1, w_cmp2, pe_cmp, w_branch, w_out):
    assert w_in.shape[0] == 1 and x_sample.shape[1] == 1, "one layer, one new token per sequence"
    b, t, d = x_prompt.shape
    n_s, n_pages = page_table.shape
    past = n_pages * PAGE_SIZE
    wq, wkv, w_all = _split_weights(w_in[0])
    w1bd, w2bd, pe_t = _compress_weights(w_cmp1[0], w_cmp2[0], pe_cmp[0])
    wb, wo = w_branch[0].astype(BF16), w_out[0].astype(BF16)

    (nq, nsilu, fq, fsilu, mq, msilu, bgate, ngate, cmp_t, slc_t, win_t, fkv_t, flog_t) = _proj_prompt(
        x_prompt, g_pre, wq, wkv, _rope_tables(jnp.arange(t, dtype=F32)), b_fgate[0])
    cb_p = _compress(cmp_t.reshape(b, 2, NSA_KV, HD, t), jnp.zeros((1, 1), jnp.int32), w1bd, w2bd, pe_t,
                     n_batch=b, n_pages=t // PAGE_SIZE, paged=False)
    cb_p = cb_p.reshape(b, 2, NSA_KV, t // CMP_BLOCK, HD)
    a_nsa = _nsa_prompt(nq, slc_t, win_t, cb_p, cb_p, ngate, nsilu)
    a_fox = _fox_prompt(fq, fkv_t, flog_t, fsilu)
    mkv_p = _mem_kv(mem_prompt, g_mem, w_mem_kv[0].astype(BF16))
    a_mem = _mem_attn(mq, msilu, mkv_p)
    y_p = _merge(a_nsa, a_fox, a_mem, bgate, x_prompt, wb, wo, g_post)

    rows_minor = lambda a: jnp.transpose(a, (0, 2, 3, 4, 1))
    hs = _proj_sample(x_sample[:, 0, :], g_pre, w_all, _rope_tables(jnp.full((n_s,), past, F32)), b_fgate[0])
    hs3 = hs.reshape(n_s, 1, N_IN_PAD)
    cb_s = _compress(rows_minor(cache_nsa_cmp[0]), page_table, w1bd, w2bd, pe_t,
                     n_batch=n_s, n_pages=n_pages, paged=True)
    cb_s = cb_s.reshape(n_s, 2, NSA_KV, past // CMP_BLOCK, HD)
    o_c, idx = _nsa_sample_sel(hs3, cb_s)
    a_nsa_s, new_win = _nsa_sample_attn(idx.reshape(n_s, NSA_KV * LANES), page_table, hs3, o_c,
                                        rows_minor(state_nsa_win[0]), rows_minor(cache_nsa_slc[0]))
    a_fox_s = _fox_sample(page_table, hs3, rows_minor(cache_fox_kv[0]),
                          jnp.transpose(cache_fox_logf[0], (0, 2, 1)))
    col = lambda k: hs[:, _S_OFF[k]:_S_OFF[k] + _COLS[k][1] - _COLS[k][0]]
    a_mem_s = _mem_attn(col("mq").reshape(n_s, 1, BR_W), col("msilu").reshape(n_s, 1, BR_W),
                        cache_mem_kv[0].reshape(n_s, -1, 2 * BR_W))
    y_s = _merge(a_nsa_s.reshape(1, n_s, BR_W), a_fox_s.reshape(1, n_s, BR_W), a_mem_s.reshape(1, n_s, BR_W),
                 col("bgate").reshape(1, n_s, 3 * d), x_sample.reshape(1, n_s, d), wb, wo, g_post)

    def rows_major(a_t, heads):
        n, _, r = a_t.shape
        return jnp.transpose(a_t.reshape(n, 2, heads, HD, r), (0, 4, 1, 2, 3))[None]

    wkeep = min(WINDOW, t)
    return (y_p, y_s.reshape(n_s, 1, d),
            rows_major(cmp_t, NSA_KV), rows_major(slc_t, NSA_KV), rows_major(win_t[:, :, t - wkeep:], NSA_KV),
            rows_major(fkv_t, FOX_H), jnp.transpose(flog_t, (0, 2, 1))[None],
            mkv_p.reshape(1, b, -1, 2, MEM_H, MEM_HD),
            col("ncmp").reshape(1, n_s, 1, 2, NSA_KV, HD), col("nslc").reshape(1, n_s, 1, 2, NSA_KV, HD),
            jnp.transpose(new_win, (0, 4, 1, 2, 3))[None],
            col("fkv").reshape(1, n_s, 1, 2, FOX_H, HD), col("flog").reshape(1, n_s, 1, FOX_H))
```

```python
import functools

import numpy as np
import jax
import jax.numpy as jnp
from jax import lax
from jax.experimental import pallas as pl
from jax.experimental.pallas import tpu as pltpu

HD = 64
NSA_H = 8
NSA_KV = 2
NSA_R = NSA_H // NSA_KV
CMP_BLOCK = 64
N_SELECT = 16
WINDOW = 512
FOX_H = 8
MEM_H = 4
MEM_HD = 128
BR_W = 512
ROT_DIM = HD // 4
ROT_HALF = ROT_DIM // 2
ROPE_THETA = 500000.0
PAGE_SIZE = 128
EPS = 1e-6
NEG = -1e30
BIG = 1e4
TINY = 1e-30
KV_W = NSA_KV * HD
SCALE = HD ** -0.5

LANES = 128
SUBLANES = 8
VMEM_LIMIT = 56 * 1024 * 1024

_COLS = dict(nq=(0, 512), ncmp=(512, 768), nslc=(768, 1024), nwin=(1024, 1280), ngate=(1280, 1304),
             nsilu=(1304, 1816), fq=(1816, 2328), fkv=(2328, 3352), flog=(3352, 3360), fsilu=(3360, 3872),
             mq=(3872, 4384), msilu=(4384, 4896), bgate=(4896, 7968))
N_IN = 7968
N_IN_PAD = 8064

F32 = jnp.float32
BF16 = jnp.bfloat16


def _nt_dot(a, b):
    return lax.dot_general(a, b, (((1,), (1,)), ((), ())), preferred_element_type=F32)


def _nn_dot(a, b):
    return lax.dot_general(a, b, (((1,), (0,)), ((), ())), preferred_element_type=F32)


def _rmsnorm_rows(x, g):
    var = jnp.mean(x * x, axis=-1, keepdims=True)
    return x * lax.rsqrt(var + EPS) * g


def _silu(x):
    return x * (1.0 / (1.0 + jnp.exp(-x)))


def _sigmoid(x):
    return 1.0 / (1.0 + jnp.exp(-x))


def _rope_tables(pos):
    inv_freq = ROPE_THETA ** (-jnp.arange(ROT_HALF, dtype=F32) / ROT_HALF)
    ang = pos[:, None] * inv_freq[None, :]
    cos, sin = jnp.cos(ang), jnp.sin(ang)
    t = pos.shape[0]
    one = jnp.ones((t, HD - ROT_DIM), F32)
    zero = jnp.zeros((t, HD - ROT_HALF), F32)
    c_head = jnp.concatenate([cos, cos, one], axis=1)
    s_hi = jnp.concatenate([-sin, zero], axis=1)
    s_lo = jnp.concatenate([jnp.zeros((t, ROT_HALF), F32), sin, jnp.zeros((t, HD - ROT_DIM), F32)], axis=1)
    tile2 = lambda a: jnp.concatenate([a, a], axis=1)
    return tile2(c_head), tile2(s_hi), tile2(s_lo), cos, sin


def _rope_lanes(x, c, s_hi, s_lo):
    k = x.shape[1] // LANES
    if k > 1:
        c = jnp.concatenate([c] * k, axis=1)
        s_hi = jnp.concatenate([s_hi] * k, axis=1)
        s_lo = jnp.concatenate([s_lo] * k, axis=1)
    n = x.shape[1]
    up = pltpu.roll(x, n - ROT_HALF, axis=1)
    dn = pltpu.roll(x, ROT_HALF, axis=1)
    return x * c + up * s_hi + dn * s_lo


_QGROUPS = ("nq", "nsilu", "fq", "fsilu", "mq", "msilu")
_KVGROUPS = (("ncmp", 256), ("nslc", 256), ("nwin", 256), ("fkv", 1024), ("flog", 8))
NQ_ROWS = 6 * 512 + 3072 + LANES
NKV_ROWS = 256 * 3 + 1024 + 8
PROJ_TM = 512


def _proj_prompt_kernel(x_ref, g_ref, wq_ref, wkv_ref, c_ref, shi_ref, slo_ref, cosT_ref, sinT_ref, bf_ref,
                        nq_ref, nsilu_ref, fq_ref, fsilu_ref, mq_ref, msilu_ref, bgate_ref, ngate_ref,
                        cmpT_ref, slcT_ref, winT_ref, fkvT_ref, flogT_ref):
    xb = _rmsnorm_rows(x_ref[0], g_ref[...]).astype(BF16)
    outs = (nq_ref, nsilu_ref, fq_ref, fsilu_ref, mq_ref, msilu_ref)
    for i, o_ref in enumerate(outs):
        h = _nt_dot(xb, wq_ref[i * 512:(i + 1) * 512, :])
        if i == 0:
            h = _rope_lanes(h, c_ref[...], shi_ref[...], slo_ref[...])
        if i in (0, 2):
            h = h * SCALE
        o_ref[0] = h.astype(o_ref.dtype)
    off = 6 * 512
    for j in range(3072 // 512):
        h = _nt_dot(xb, wq_ref[off + j * 512: off + (j + 1) * 512, :])
        bgate_ref[0, :, j * 512:(j + 1) * 512] = h.astype(bgate_ref.dtype)
    off += 3072
    ngate_ref[0] = _nt_dot(xb, wq_ref[off:off + LANES, :])
    cosT, sinT = cosT_ref[...], sinT_ref[...]
    roff = 0
    for o_ref in (cmpT_ref, slcT_ref, winT_ref):
        hT = _nt_dot(wkv_ref[roff:roff + 256, :], xb)
        o_ref[0] = hT
        for g in range(NSA_KV):
            x1 = hT[g * HD: g * HD + ROT_HALF, :]
            x2 = hT[g * HD + ROT_HALF: g * HD + ROT_DIM, :]
            o_ref[0, g * HD: g * HD + ROT_HALF, :] = x1 * cosT - x2 * sinT
            o_ref[0, g * HD + ROT_HALF: g * HD + ROT_DIM, :] = x2 * cosT + x1 * sinT
        roff += 256
    for j in range(1024 // 256):
        fkvT_ref[0, j * 256:(j + 1) * 256, :] = _nt_dot(wkv_ref[roff + j * 256: roff + (j + 1) * 256, :], xb)
    roff += 1024
    flogT_ref[0] = _log_sigmoid(_nt_dot(wkv_ref[roff:roff + 8, :], xb) + bf_ref[...])


def _proj_prompt(x, g_pre, wq, wkv, tabs, b_fgate):
    b, t, d = x.shape
    tm = min(PROJ_TM, t)
    nt = t // tm
    c, shi, slo, cos, sin = tabs
    cosT, sinT = cos.T, sin.T
    row = lambda bi, ti: (bi, ti, 0)
    col = lambda bi, ti: (bi, 0, ti)
    const = lambda bi, ti: (0, 0)
    tab = lambda bi, ti: (ti, 0)
    tabT = lambda bi, ti: (0, ti)
    resident = dict(pipeline_mode=pl.Buffered(1))
    in_specs = [
        pl.BlockSpec((1, tm, d), row),
        pl.BlockSpec((1, d), const),
        pl.BlockSpec((NQ_ROWS, d), const, **resident),
        pl.BlockSpec((NKV_ROWS, d), const, **resident),
        pl.BlockSpec((tm, LANES), tab), pl.BlockSpec((tm, LANES), tab), pl.BlockSpec((tm, LANES), tab),
        pl.BlockSpec((ROT_HALF, tm), tabT), pl.BlockSpec((ROT_HALF, tm), tabT),
        pl.BlockSpec((FOX_H, 1), const),
    ]
    q_shape = jax.ShapeDtypeStruct((b, t, 512), BF16)
    out_shape = [q_shape] * 6 + [
        jax.ShapeDtypeStruct((b, t, 3072), BF16),
        jax.ShapeDtypeStruct((b, t, LANES), F32),
        jax.ShapeDtypeStruct((b, 256, t), F32), jax.ShapeDtypeStruct((b, 256, t), F32),
        jax.ShapeDtypeStruct((b, 256, t), F32),
        jax.ShapeDtypeStruct((b, 1024, t), F32),
        jax.ShapeDtypeStruct((b, 8, t), F32),
    ]
    out_specs = [pl.BlockSpec((1, tm, 512), row)] * 6 + [
        pl.BlockSpec((1, tm, 3072), row),
        pl.BlockSpec((1, tm, LANES), row),
        pl.BlockSpec((1, 256, tm), col), pl.BlockSpec((1, 256, tm), col), pl.BlockSpec((1, 256, tm), col),
        pl.BlockSpec((1, 1024, tm), col),
        pl.BlockSpec((1, 8, tm), col),
    ]
    return pl.pallas_call(
        _proj_prompt_kernel,
        grid=(b, nt),
        in_specs=in_specs,
        out_specs=out_specs,
        out_shape=out_shape,
        compiler_params=pltpu.CompilerParams(dimension_semantics=("arbitrary", "arbitrary"),
                                             vmem_limit_bytes=VMEM_LIMIT),
        name="proj_prompt",
    )(x, g_pre, wq, wkv, c, shi, slo, cosT, sinT, b_fgate.reshape(FOX_H, 1))


PROJ_S_TN = 1152


def _proj_sample_kernel(x_ref, g_ref, w_ref, c_ref, shi_ref, slo_ref, bf_ref, o_ref):
    xb = _rmsnorm_rows(x_ref[...], g_ref[...]).astype(BF16)
    o_ref[...] = _nt_dot(xb, w_ref[...])

    @pl.when(pl.program_id(0) == _S_OFF["flog"] // PROJ_S_TN)
    def _():
        lo = _S_OFF["flog"] % PROJ_S_TN
        o_ref[:, lo:lo + FOX_H] = _log_sigmoid(o_ref[:, lo:lo + FOX_H] + bf_ref[...])

    @pl.when(pl.program_id(0) == 0)
    def _():
        c, shi, slo = c_ref[...], shi_ref[...], slo_ref[...]
        for c0 in (0, 128, 256, 384, 512, 768, 1024):
            o_ref[:, c0:c0 + LANES] = _rope_lanes(o_ref[:, c0:c0 + LANES], c, shi, slo)


def _proj_sample(x, g_pre, w_all, tabs, b_fgate):
    n, d = x.shape
    c, shi, slo, _, _ = tabs
    const = lambda j: (0, 0)
    return pl.pallas_call(
        _proj_sample_kernel,
        grid=(N_IN_PAD // PROJ_S_TN,),
        in_specs=[pl.BlockSpec((n, d), const), pl.BlockSpec((1, d), const),
                  pl.BlockSpec((PROJ_S_TN, d), lambda j: (j, 0)),
                  pl.BlockSpec((n, LANES), const), pl.BlockSpec((n, LANES), const), pl.BlockSpec((n, LANES), const),
                  pl.BlockSpec((1, FOX_H), const)],
        out_specs=pl.BlockSpec((n, PROJ_S_TN), lambda j: (0, j)),
        out_shape=jax.ShapeDtypeStruct((n, N_IN_PAD), F32),
        compiler_params=pltpu.CompilerParams(dimension_semantics=("arbitrary",)),
        name="proj_sample",
    )(x, g_pre, w_all, c, shi, slo, b_fgate.reshape(1, FOX_H))


CMP_PITCH = 72
CMP_DCHUNK = 8


def _compress_kernel(pt_ref, src_ref, w1_ref, w2_ref, pe_ref, o_ref, buf, sem, *, n_pages, paged):
    bi, kv = pl.program_id(0), pl.program_id(1)
    n_steps = pl.num_programs(0) * 2
    step = bi * 2 + kv
    slot = step % 2

    def page_copy(b_, kv_, i, slot_):
        if paged:
            src = src_ref.at[pt_ref[b_, i], kv_]
        else:
            src = src_ref.at[b_, kv_, :, :, pl.ds(i * PAGE_SIZE, PAGE_SIZE)]
        dst = buf.at[slot_, :, pl.ds(i * CMP_PITCH, HD), :]
        return pltpu.make_async_copy(src, dst, sem.at[slot_])

    def issue(b_, kv_, slot_):
        for i in range(n_pages):
            page_copy(b_, kv_, i, slot_).start()

    @pl.when(step == 0)
    def _():
        issue(bi, kv, slot)

    @pl.when(step + 1 < n_steps)
    def _():
        nxt = step + 1
        issue(nxt // 2, nxt % 2, 1 - slot)

    for i in range(n_pages):
        page_copy(0, 0, i, slot).wait()

    acc = jnp.zeros((2 * n_pages, LANES), F32)
    for c in range(HD // CMP_DCHUNK):
        cols = []
        for dd in range(CMP_DCHUNK):
            d = c * CMP_DCHUNK + dd
            rows = [buf[slot, g, pl.ds(d, n_pages, stride=CMP_PITCH), :] for g in range(NSA_KV)]
            x = jnp.concatenate(rows, axis=0) + pe_ref[0, d:d + 1, :]
            cols.append(x.astype(BF16))
        lhs = jnp.concatenate(cols, axis=1)
        k0 = c * CMP_DCHUNK * LANES
        acc = acc + _nn_dot(lhs, w1_ref[0, k0:k0 + CMP_DCHUNK * LANES, :])
    hid = _silu(acc)
    out = _nn_dot(hid.astype(BF16), w2_ref[0])
    o_ref[0, 0, 0] = out[:n_pages]
    o_ref[0, 0, 1] = out[n_pages:]


def _compress(src, page_table, w1bd, w2bd, pe_t, *, n_batch, n_pages, paged):
    kern = functools.partial(_compress_kernel, n_pages=n_pages, paged=paged)
    grid_spec = pltpu.PrefetchScalarGridSpec(
        num_scalar_prefetch=1,
        grid=(n_batch, 2),
        in_specs=[pl.BlockSpec(memory_space=pl.ANY),
                  pl.BlockSpec((1, HD * LANES, LANES), lambda b, kv, pt: (kv, 0, 0)),
                  pl.BlockSpec((1, LANES, LANES), lambda b, kv, pt: (kv, 0, 0)),
                  pl.BlockSpec((1, HD, LANES), lambda b, kv, pt: (kv, 0, 0))],
        out_specs=pl.BlockSpec((1, 1, NSA_KV, n_pages, LANES), lambda b, kv, pt: (b, kv, 0, 0, 0)),
        scratch_shapes=[pltpu.VMEM((2, NSA_KV, n_pages * CMP_PITCH, LANES), F32),
                        pltpu.SemaphoreType.DMA((2,))],
    )
    return pl.pallas_call(
        kern, grid_spec=grid_spec,
        out_shape=jax.ShapeDtypeStruct((n_batch, 2, NSA_KV, n_pages, LANES), F32),
        compiler_params=pltpu.CompilerParams(dimension_semantics=("arbitrary", "arbitrary"),
                                             vmem_limit_bytes=VMEM_LIMIT),
        name="compress_paged" if paged else "compress_prompt",
    )(page_table, src, w1bd, w2bd, pe_t)


def _compress_weights(w1, w2, pe):
    eye2 = jnp.eye(2, dtype=w1.dtype)
    w1bd = jnp.einsum('klde,nm->kdnlme', w1, eye2).reshape(2, HD * LANES, LANES).astype(BF16)
    w2bd = jnp.einsum('kef,nm->knemf', w2, eye2).reshape(2, LANES, LANES).astype(BF16)
    pe_t = jnp.tile(jnp.transpose(pe, (0, 2, 1)), (1, 1, 2))
    return w1bd, w2bd, pe_t


def _log_sigmoid(x):
    return jnp.minimum(x, 0.0) - jnp.log1p(jnp.exp(-jnp.abs(x)))


def _lane_cumsum(x):
    lane = lax.broadcasted_iota(jnp.int32, x.shape, 1)
    s = 1
    while s < LANES:
        x = x + jnp.where(lane >= s, pltpu.roll(x, s, axis=1), 0.0)
        s *= 2
    return x


M_INIT = -1e29


def _online_update(carry, s, v_t):
    m, l, acc = carry
    m_new = jnp.maximum(m, jnp.max(s, axis=0, keepdims=True))
    alpha = jnp.exp(m - m_new)
    p = jnp.exp(s - m_new)
    l = l * alpha + jnp.sum(p, axis=0, keepdims=True)
    pb = p.astype(BF16)
    if isinstance(v_t, (list, tuple)):
        w = pb.shape[1] // len(v_t)
        pv = jnp.concatenate([_nn_dot(v, pb[:, j * w:(j + 1) * w]) for j, v in enumerate(v_t)], axis=1)
    else:
        pv = _nn_dot(v_t, pb)
    return m_new, l, acc * alpha + pv


def _online_init(nq, d):
    return (jnp.full((1, nq), M_INIT, F32), jnp.zeros((1, nq), F32), jnp.zeros((d, nq), F32))


def _mask_bias(cond):
    return jnp.where(cond, 0.0, NEG)


def _online_finish(carry):
    _, l, acc = carry
    return acc / jnp.maximum(l, TINY)


ATT_T = 256
FOX_HPS = 4


def _fox_prompt_kernel(q_ref, k_ref, v_ref, logf_ref, gate_ref, o_ref, k_r, v_t, c_b):
    hg, qi = pl.program_id(1), pl.program_id(2)
    tk = ATT_T
    tq = ATT_T
    t = k_ref.shape[2]
    nkt = t // tk
    nh = FOX_HPS

    @pl.when(qi == 0)
    def _():
        for kt in range(nkt):
            sl = slice(kt * tk, (kt + 1) * tk)
            k_r[kt] = k_ref[0, :, sl].T.astype(BF16)
            v_t[kt] = v_ref[0, :, sl].astype(BF16)
        carry = jnp.zeros((FOX_H, 1), F32)
        per = tk // LANES
        for ch in range(t // LANES):
            x = _lane_cumsum(logf_ref[0, :, ch * LANES:(ch + 1) * LANES]) + carry
            carry = x[:, LANES - 1:LANES]
            head = lax.broadcasted_iota(jnp.int32, x.shape, 0)
            for j in range(nh):
                xr = jnp.sum(jnp.where(head == nh * hg + j, x, 0.0), axis=0, keepdims=True)
                c_b[j, ch // per, (ch % per) * LANES:(ch % per + 1) * LANES, :] = jnp.broadcast_to(xr, (LANES, LANES)).T

    q_t = q_ref[0].astype(F32).T.astype(BF16)
    row_head = lax.broadcasted_iota(jnp.int32, q_t.shape, 0) // HD
    q_bd = jnp.concatenate([jnp.where(row_head == j, q_t, jnp.zeros_like(q_t)) for j in range(nh)], axis=1)
    c_0 = [c_b[j, qi, 0:1, :] for j in range(nh)]

    def logits(kt):
        bias = jnp.concatenate([c_b[j, kt] - c_0[j] for j in range(nh) for _ in range(tq // LANES)], axis=1)
        return _nn_dot(k_r[kt], q_bd) - bias

    def values(kt):
        return [v_t[kt, j * HD:(j + 1) * HD, :] for j in range(nh)]

    def body(kt, carry):
        return _online_update(carry, logits(kt), values(kt))

    carry = lax.fori_loop(0, qi, body, _online_init(nh * tq, HD))
    causal = _mask_bias(lax.broadcasted_iota(jnp.int32, (tk, tq), 0) <= lax.broadcasted_iota(jnp.int32, (tk, tq), 1))
    o_t = _online_finish(_online_update(carry, logits(qi) + jnp.concatenate([causal] * nh, axis=1), values(qi)))
    o = jnp.concatenate([o_t[:, j * tq:(j + 1) * tq] for j in range(nh)], axis=0).T
    o_ref[0] = (o * _silu(gate_ref[0].astype(F32))).astype(o_ref.dtype)


def _fox_prompt(fq, fkv_t, logf_t, fsilu):
    b, t, _ = fq.shape
    tq = ATT_T
    nkt = t // tq
    ng = FOX_H // FOX_HPS
    gw = FOX_HPS * HD
    return pl.pallas_call(
        _fox_prompt_kernel,
        grid=(b, ng, t // tq),
        in_specs=[pl.BlockSpec((1, tq, gw), lambda bi, hg, qi: (bi, qi, hg)),
                  pl.BlockSpec((1, gw, t), lambda bi, hg, qi: (bi, hg, 0)),
                  pl.BlockSpec((1, gw, t), lambda bi, hg, qi: (bi, ng + hg, 0)),
                  pl.BlockSpec((1, FOX_H, t), lambda bi, hg, qi: (bi, 0, 0)),
                  pl.BlockSpec((1, tq, gw), lambda bi, hg, qi: (bi, qi, hg))],
        out_specs=pl.BlockSpec((1, tq, gw), lambda bi, hg, qi: (bi, qi, hg)),
        out_shape=jax.ShapeDtypeStruct((b, t, BR_W), BF16),
        scratch_shapes=[pltpu.VMEM((nkt, tq, gw), BF16), pltpu.VMEM((nkt, gw, tq), BF16),
                        pltpu.VMEM((FOX_HPS, nkt, tq, LANES), F32)],
        compiler_params=pltpu.CompilerParams(dimension_semantics=("arbitrary",) * 3, vmem_limit_bytes=VMEM_LIMIT),
        name="fox_prompt",
    )(fq, fkv_t, fkv_t, logf_t, fsilu)


def _nsa_prompt_kernel(q_ref, ks_ref, vs_ref, kw_ref, vw_ref, kc_ref, vc_ref, ngate_ref, gate_ref, o_ref,
                       ks_r, vs_t, kw_r, vw_t):
    g, qi = pl.program_id(1), pl.program_id(2)
    tk = ATT_T
    tq = ATT_T
    t = ks_ref.shape[2]
    nkt = t // tk
    nb = t // CMP_BLOCK
    nq = NSA_R * tq

    @pl.when(qi == 0)
    def _():
        for kt in range(nkt):
            sl = slice(kt * tk, (kt + 1) * tk)
            k_rows = jnp.concatenate([ks_ref[0, :, sl], kw_ref[0, :, sl]], axis=0).T
            ks_r[kt] = k_rows[:, :HD].astype(BF16)
            kw_r[kt] = k_rows[:, HD:].astype(BF16)
            vs_t[kt] = vs_ref[0, :, sl].astype(BF16)
            vw_t[kt] = vw_ref[0, :, sl].astype(BF16)

    q_t = q_ref[0].astype(F32).T
    q4 = jnp.concatenate([q_t[r * HD:(r + 1) * HD] for r in range(NSA_R)], axis=1).astype(BF16)
    q0 = qi * tq
    tpos4 = q0 + lax.broadcasted_iota(jnp.int32, (nb, nq), 1) % tq
    blk4 = lax.broadcasted_iota(jnp.int32, (nb, nq), 0)
    readable = (blk4 + 1) * CMP_BLOCK - 1 <= tpos4
    s_c = jnp.where(readable, _nn_dot(kc_ref[0, 0, 0].astype(BF16), q4), NEG)
    m_c = jnp.max(s_c, axis=0, keepdims=True)
    e_c = jnp.where(readable, jnp.exp(s_c - m_c), 0.0)
    p_c = e_c / jnp.maximum(jnp.sum(e_c, axis=0, keepdims=True), TINY)
    eye = jnp.where(lax.broadcasted_iota(jnp.int32, (HD, HD), 0) == lax.broadcasted_iota(jnp.int32, (HD, HD), 1),
                    1.0, 0.0).astype(BF16)
    vc_t = _nt_dot(eye, vc_ref[0, 0, 0].astype(BF16)).astype(BF16)
    o_c = _nn_dot(vc_t, p_c.astype(BF16))
    imp = p_c[:, 0:tq] + p_c[:, tq:2 * tq] + p_c[:, 2 * tq:3 * tq] + p_c[:, 3 * tq:4 * tq]
    tpos = q0 + lax.broadcasted_iota(jnp.int32, (nb, tq), 1)
    blk = lax.broadcasted_iota(jnp.int32, (nb, tq), 0)
    cur = tpos // CMP_BLOCK
    score = jnp.where(blk < cur, imp, jnp.where(blk == cur, BIG, NEG))
    rank = jnp.zeros((nb, tq), F32)
    for mth in range(nb):
        rowv = score[mth:mth + 1, :]
        ahead = (rowv > score) | ((rowv == score) & (blk > mth))
        rank = rank + jnp.where(ahead, 1.0, 0.0)
    sel_bias = _mask_bias((rank < N_SELECT) & (score > 0.5 * NEG)).astype(BF16)

    key = lax.broadcasted_iota(jnp.int32, (tk, tq), 0)
    qry = lax.broadcasted_iota(jnp.int32, (tk, tq), 1)
    heads = lambda a: jnp.concatenate([a] * NSA_R, axis=1)

    def sel_logits(kt):
        kb = (kt * tk + lax.broadcasted_iota(jnp.int32, (tk, nb), 0)) // CMP_BLOCK
        expand = jnp.where(kb == lax.broadcasted_iota(jnp.int32, (tk, nb), 1), 1.0, 0.0).astype(BF16)
        bias = _nn_dot(expand, sel_bias)
        return _nn_dot(ks_r[kt], q4), bias

    def sel_body(kt, carry):
        s, bias = sel_logits(kt)
        return _online_update(carry, s + heads(bias), vs_t[kt])

    carry = lax.fori_loop(0, qi, sel_body, _online_init(nq, HD))
    s, bias = sel_logits(qi)
    carry = _online_update(carry, s + heads(bias + _mask_bias(key <= qry)), vs_t[qi])
    o_s = _online_finish(carry)
    carry = _online_init(nq, HD)
    k2 = jnp.maximum(qi - 2, 0)
    carry = _online_update(carry, _nn_dot(kw_r[k2], q4) + heads(_mask_bias((key >= qry) & (qi >= 2))), vw_t[k2])
    k1 = jnp.maximum(qi - 1, 0)
    carry = _online_update(carry, _nn_dot(kw_r[k1], q4) + _mask_bias(qi >= 1), vw_t[k1])
    carry = _online_update(carry, _nn_dot(kw_r[qi], q4) + heads(_mask_bias(key <= qry)), vw_t[qi])
    o_w = _online_finish(carry)
    gts = _sigmoid(ngate_ref[0]).T

    def gate_row(br, r):
        c0 = br * NSA_H + r
        return jnp.where(g == 0, gts[c0:c0 + 1, :], gts[c0 + NSA_R:c0 + NSA_R + 1, :])

    outs = []
    for r in range(NSA_R):
        sl = slice(r * tq, (r + 1) * tq)
        outs.append(gate_row(0, r) * o_c[:, sl] + gate_row(1, r) * o_s[:, sl] + gate_row(2, r) * o_w[:, sl])
    o = jnp.concatenate(outs, axis=0).T * _silu(gate_ref[0].astype(F32))
    o_ref[0] = o.astype(o_ref.dtype)


def _nsa_prompt(nq, slc_t, win_t, kcb, vcb, ngate, nsilu):
    b, t, _ = nq.shape
    tq = ATT_T
    nkt = t // tq
    nb = t // CMP_BLOCK
    gw = NSA_R * HD
    kv_spec = lambda kv: pl.BlockSpec((1, HD, t), lambda bi, g, qi, kv=kv: (bi, kv * NSA_KV + g, 0))
    cb_spec = lambda kv: pl.BlockSpec((1, 1, 1, nb, HD), lambda bi, g, qi, kv=kv: (bi, kv, g, 0, 0))
    return pl.pallas_call(
        _nsa_prompt_kernel,
        grid=(b, NSA_KV, t // tq),
        in_specs=[pl.BlockSpec((1, tq, gw), lambda bi, g, qi: (bi, qi, g)),
                  kv_spec(0), kv_spec(1), kv_spec(0), kv_spec(1),
                  cb_spec(0), cb_spec(1),
                  pl.BlockSpec((1, tq, LANES), lambda bi, g, qi: (bi, qi, 0)),
                  pl.BlockSpec((1, tq, gw), lambda bi, g, qi: (bi, qi, g))],
        out_specs=pl.BlockSpec((1, tq, gw), lambda bi, g, qi: (bi, qi, g)),
        out_shape=jax.ShapeDtypeStruct((b, t, BR_W), BF16),
        scratch_shapes=[pltpu.VMEM((nkt, tq, HD), BF16), pltpu.VMEM((nkt, HD, tq), BF16),
                        pltpu.VMEM((nkt, tq, HD), BF16), pltpu.VMEM((nkt, HD, tq), BF16)],
        compiler_params=pltpu.CompilerParams(dimension_semantics=("arbitrary",) * 3, vmem_limit_bytes=VMEM_LIMIT),
        name="nsa_prompt",
    )(nq, slc_t, slc_t, win_t, win_t, kcb, vcb, ngate, nsilu)


def _mem_kv_kernel(m_ref, g_ref, w_ref, o_ref):
    o_ref[0] = _nn_dot(_rmsnorm_rows(m_ref[0], g_ref[...]).astype(BF16), w_ref[...])


def _mem_kv(mem, g_mem, w_bf):
    b, m, d = mem.shape
    n = w_bf.shape[1]
    return pl.pallas_call(
        _mem_kv_kernel, grid=(b,),
        in_specs=[pl.BlockSpec((1, m, d), lambda bi: (bi, 0, 0)), pl.BlockSpec((1, d), lambda bi: (0, 0)),
                  pl.BlockSpec((d, n), lambda bi: (0, 0))],
        out_specs=pl.BlockSpec((1, m, n), lambda bi: (bi, 0, 0)),
        out_shape=jax.ShapeDtypeStruct((b, m, n), F32),
        compiler_params=pltpu.CompilerParams(dimension_semantics=("arbitrary",)),
        name="mem_kv",
    )(mem, g_mem, w_bf)


MEM_SCALE = MEM_HD ** -0.5


def _mem_attn_kernel(mq_ref, ms_ref, mkv_ref, o_ref):
    mq = mq_ref[0].astype(BF16)
    outs = []
    for h in range(MEM_H):
        kh = mkv_ref[0, :, h * MEM_HD:(h + 1) * MEM_HD].astype(BF16)
        vh = mkv_ref[0, :, BR_W + h * MEM_HD: BR_W + (h + 1) * MEM_HD].astype(BF16)
        s = _nt_dot(mq[:, h * MEM_HD:(h + 1) * MEM_HD], kh) * MEM_SCALE
        e = jnp.exp(s - jnp.max(s, axis=1, keepdims=True))
        p = e / jnp.sum(e, axis=1, keepdims=True)
        outs.append(_nn_dot(p.astype(BF16), vh))
    o_ref[0] = (jnp.concatenate(outs, axis=1) * _silu(ms_ref[0].astype(F32))).astype(o_ref.dtype)


MERGE_TM = 512


def _mem_attn(mq, msilu, mkv):
    b, t, _ = mq.shape
    tm = min(MERGE_TM, t)
    m = mkv.shape[1]
    row = lambda bi, ti: (bi, ti, 0)
    return pl.pallas_call(
        _mem_attn_kernel, grid=(b, t // tm),
        in_specs=[pl.BlockSpec((1, tm, BR_W), row), pl.BlockSpec((1, tm, BR_W), row),
                  pl.BlockSpec((1, m, 2 * BR_W), lambda bi, ti: (bi, 0, 0))],
        out_specs=pl.BlockSpec((1, tm, BR_W), row),
        out_shape=jax.ShapeDtypeStruct((b, t, BR_W), BF16),
        compiler_params=pltpu.CompilerParams(dimension_semantics=("arbitrary", "arbitrary")),
        name="mem_attn",
    )(mq, msilu, mkv)


def _merge_kernel(an_ref, af_ref, am_ref, bg_ref, x_ref, wb_ref, wo_ref, gp_ref, o_ref):
    d = wo_ref.shape[0]
    merged = None
    for k, a_ref in enumerate((an_ref, af_ref, am_ref)):
        z = _nn_dot(a_ref[0].astype(BF16), wb_ref[k])
        term = _sigmoid(bg_ref[0, :, k * d:(k + 1) * d].astype(F32)) * z
        merged = term if merged is None else merged + term
    y = _nn_dot(merged.astype(BF16), wo_ref[...])
    o_ref[0] = x_ref[0] + _rmsnorm_rows(y, gp_ref[...])


def _merge(a_nsa, a_fox, a_mem, bgate, x, wb, wo, g_post):
    b, t, d = x.shape
    tm = min(MERGE_TM, t)
    row = lambda bi, ti: (bi, ti, 0)
    resident = dict(pipeline_mode=pl.Buffered(1))
    return pl.pallas_call(
        _merge_kernel, grid=(b, t // tm),
        in_specs=[pl.BlockSpec((1, tm, BR_W), row), pl.BlockSpec((1, tm, BR_W), row), pl.BlockSpec((1, tm, BR_W), row),
                  pl.BlockSpec((1, tm, 3 * d), row), pl.BlockSpec((1, tm, d), row),
                  pl.BlockSpec((3, BR_W, d), lambda bi, ti: (0, 0, 0), **resident),
                  pl.BlockSpec((d, d), lambda bi, ti: (0, 0), **resident),
                  pl.BlockSpec((1, d), lambda bi, ti: (0, 0))],
        out_specs=pl.BlockSpec((1, tm, d), row),
        out_shape=jax.ShapeDtypeStruct((b, t, d), F32),
        compiler_params=pltpu.CompilerParams(dimension_semantics=("arbitrary", "arbitrary"),
                                             vmem_limit_bytes=VMEM_LIMIT),
        name="merge",
    )(a_nsa, a_fox, a_mem, bgate, x, wb, wo, g_post)


_S_ORDER = ("nq", "ncmp", "nslc", "nwin", "nsilu", "fq", "fkv", "fsilu", "mq", "msilu", "bgate", "ngate", "flog")
_S_OFF = {}
_o = 0
for _k in _S_ORDER:
    _S_OFF[_k] = _o
    _o += _COLS[_k][1] - _COLS[_k][0]
assert _o == N_IN and _S_OFF["ngate"] % LANES == 0
N_PAST_SEL = N_SELECT - 1


def _row_to_col(row):
    n = row.shape[1]
    eye = lax.broadcasted_iota(jnp.int32, (n, n), 0) == lax.broadcasted_iota(jnp.int32, (n, n), 1)
    return jnp.sum(jnp.where(eye, row, 0.0), axis=1, keepdims=True)


def _col_to_row(col):
    n = col.shape[0]
    eye = lax.broadcasted_iota(jnp.int32, (n, n), 0) == lax.broadcasted_iota(jnp.int32, (n, n), 1)
    return jnp.sum(jnp.where(eye, col, 0.0), axis=0, keepdims=True)


def _hrow(hs_ref, key, off=0, n=None):
    lo = _S_OFF[key] + off
    n = (_COLS[key][1] - _COLS[key][0] - off) if n is None else n
    return hs_ref[0, :, lo:lo + n]


def _q_group(hs_ref, key, g):
    return jnp.concatenate([_hrow(hs_ref, key, (g * NSA_R + r) * HD, HD) for r in range(NSA_R)], axis=0) * SCALE


def _nsa_sample_sel_kernel(hs_ref, cb_ref, oc_ref, idx_ref):
    nblk = cb_ref.shape[3]
    idx_ref[...] = jnp.zeros(idx_ref.shape, idx_ref.dtype)
    for g in range(NSA_KV):
        q4 = _q_group(hs_ref, "nq", g).astype(BF16)
        s = _nt_dot(q4, cb_ref[0, 0, g].astype(BF16))
        e = jnp.exp(s - jnp.max(s, axis=1, keepdims=True))
        p = e / jnp.sum(e, axis=1, keepdims=True)
        oc_ref[0, g * NSA_R:(g + 1) * NSA_R, :] = _nn_dot(p.astype(BF16), cb_ref[0, 1, g].astype(BF16))
        imp = jnp.sum(p, axis=0, keepdims=True)
        col = _row_to_col(imp)
        mi = lax.broadcasted_iota(jnp.int32, (nblk, nblk), 0)
        ni = lax.broadcasted_iota(jnp.int32, (nblk, nblk), 1)
        ahead = (col > imp) | ((col == imp) & (mi < ni))
        rank = jnp.sum(jnp.where(ahead, 1.0, 0.0), axis=0, keepdims=True)
        sel = jnp.where(rank < N_PAST_SEL, 1.0, 0.0)
        pos, carry = [], jnp.zeros((1, 1), F32)
        for ch in range(nblk // LANES):
            chunk = sel[:, ch * LANES:(ch + 1) * LANES]
            incl = _lane_cumsum(chunk) + carry
            carry = incl[:, LANES - 1:LANES]
            pos.append(incl - chunk)
        pos = jnp.concatenate(pos, axis=1)
        slot = lax.broadcasted_iota(jnp.int32, (N_SELECT, nblk), 0).astype(F32)
        onehot = jnp.where((pos == slot) & (sel > 0.5), 1.0, 0.0).astype(BF16)
        ids = lax.broadcasted_iota(jnp.int32, (1, nblk), 1).astype(F32).astype(BF16)
        idx = _nt_dot(ids, onehot)
        idx_ref[0, g:g + 1, 0:N_SELECT] = idx.astype(jnp.int32)


def _nsa_sample_sel(hs3, cb):
    n = hs3.shape[0]
    nblk = cb.shape[3]
    assert nblk % LANES == 0 and nblk >= N_PAST_SEL and nblk <= 256
    return pl.pallas_call(
        _nsa_sample_sel_kernel, grid=(n,),
        in_specs=[pl.BlockSpec((1, 1, N_IN_PAD), lambda b: (b, 0, 0)),
                  pl.BlockSpec((1, 2, NSA_KV, nblk, HD), lambda b: (b, 0, 0, 0, 0))],
        out_specs=[pl.BlockSpec((1, NSA_H, HD), lambda b: (b, 0, 0)),
                   pl.BlockSpec((1, NSA_KV, LANES), lambda b: (b, 0, 0))],
        out_shape=[jax.ShapeDtypeStruct((n, NSA_H, HD), F32), jax.ShapeDtypeStruct((n, NSA_KV, LANES), jnp.int32)],
        compiler_params=pltpu.CompilerParams(dimension_semantics=("arbitrary",)),
        name="nsa_sample_select",
    )(hs3, cb)


def _nsa_sample_attn_kernel(idx_ref, pt_ref, hs_ref, oc_ref, win_ref, *refs):
    n_blk = NSA_KV * N_PAST_SEL
    blk_refs, (o_ref, nw_ref) = refs[:n_blk], refs[n_blk:]
    b = pl.program_id(0)
    gates = _sigmoid(_hrow(hs_ref, "ngate"))
    lane_half = lax.broadcasted_iota(jnp.int32, (NSA_R, PAGE_SIZE), 1) // CMP_BLOCK
    outs = []
    for g in range(NSA_KV):
        q4 = _q_group(hs_ref, "nq", g)
        q4b = q4.astype(BF16)
        kn, vn = _hrow(hs_ref, "nslc", g * HD, HD), _hrow(hs_ref, "nslc", KV_W + g * HD, HD)
        s_cur = jnp.sum(q4 * kn, axis=1, keepdims=True)
        s_list, masks = [], []
        for k in range(N_PAST_SEL):
            half = idx_ref[b, g * LANES + k] & 1
            mask = lane_half == half
            s = _nn_dot(q4b, blk_refs[g * N_PAST_SEL + k][0, 0, 0].astype(BF16))
            s_list.append(jnp.where(mask, s, NEG))
            masks.append(mask)
        m = s_cur
        for s in s_list:
            m = jnp.maximum(m, jnp.max(s, axis=1, keepdims=True))
        l = jnp.exp(s_cur - m)
        acc = l * vn
        for k in range(N_PAST_SEL):
            e = jnp.where(masks[k], jnp.exp(s_list[k] - m), 0.0)
            l = l + jnp.sum(e, axis=1, keepdims=True)
            acc = acc + _nt_dot(e.astype(BF16), blk_refs[g * N_PAST_SEL + k][0, 1, 0].astype(BF16))
        o_s = acc / jnp.maximum(l, TINY)
        kwn, vwn = _hrow(hs_ref, "nwin", g * HD, HD), _hrow(hs_ref, "nwin", KV_W + g * HD, HD)
        s_w = _nn_dot(q4b, win_ref[0, 0, g].astype(BF16))
        s_wn = jnp.sum(q4 * kwn, axis=1, keepdims=True)
        m = jnp.maximum(jnp.max(s_w, axis=1, keepdims=True), s_wn)
        e, en = jnp.exp(s_w - m), jnp.exp(s_wn - m)
        l = jnp.sum(e, axis=1, keepdims=True) + en
        o_w = (_nt_dot(e.astype(BF16), win_ref[0, 1, g].astype(BF16)) + en * vwn) / jnp.maximum(l, TINY)
        o_c = oc_ref[0, g * NSA_R:(g + 1) * NSA_R, :]
        for r in range(NSA_R):
            c = g * NSA_R + r
            gate = lambda br: gates[:, br * NSA_H + c: br * NSA_H + c + 1]
            outs.append(gate(0) * o_c[r:r + 1] + gate(1) * o_s[r:r + 1] + gate(2) * o_w[r:r + 1])
    o_ref[0] = jnp.concatenate(outs, axis=1) * _silu(_hrow(hs_ref, "nsilu"))
    wb = win_ref.shape[4]
    newcol = _row_to_col(_hrow(hs_ref, "nwin"))
    last = lax.broadcasted_iota(jnp.int32, (HD, wb), 1) == wb - 1
    for kv in range(2):
        for g in range(NSA_KV):
            r0 = (kv * NSA_KV + g) * HD
            shifted = pltpu.roll(win_ref[0, kv, g], wb - 1, axis=1)
            nw_ref[0, kv, g] = jnp.where(last, newcol[r0:r0 + HD], shifted)


def _nsa_sample_attn(idx, page_table, hs3, o_c, win_state, pool_slc):
    n = hs3.shape[0]
    wb = win_state.shape[4]
    assert wb == WINDOW
    blk_specs = [pl.BlockSpec((1, 2, 1, HD, PAGE_SIZE),
                              lambda b, idx, pt, g=g, k=k: (pt[b, idx[b, g * LANES + k] >> 1], 0, g, 0, 0))
                 for g in range(NSA_KV) for k in range(N_PAST_SEL)]
    grid_spec = pltpu.PrefetchScalarGridSpec(
        num_scalar_prefetch=2, grid=(n,),
        in_specs=[pl.BlockSpec((1, 1, N_IN_PAD), lambda b, idx, pt: (b, 0, 0)),
                  pl.BlockSpec((1, NSA_H, HD), lambda b, idx, pt: (b, 0, 0)),
                  pl.BlockSpec((1, 2, NSA_KV, HD, wb), lambda b, idx, pt: (b, 0, 0, 0, 0))] + blk_specs,
        out_specs=[pl.BlockSpec((1, 1, BR_W), lambda b, idx, pt: (b, 0, 0)),
                   pl.BlockSpec((1, 2, NSA_KV, HD, wb), lambda b, idx, pt: (b, 0, 0, 0, 0))],
    )
    return pl.pallas_call(
        _nsa_sample_attn_kernel, grid_spec=grid_spec,
        out_shape=[jax.ShapeDtypeStruct((n, 1, BR_W), F32), jax.ShapeDtypeStruct((n, 2, NSA_KV, HD, wb), F32)],
        compiler_params=pltpu.CompilerParams(dimension_semantics=("arbitrary",)),
        name="nsa_sample_attn",
    )(idx, page_table, hs3, o_c, win_state, *([pool_slc] * (NSA_KV * N_PAST_SEL)))


FOX_NPG = 8


def _fox_sample_kernel(pt_ref, hs_ref, *refs):
    kv_refs = refs[:FOX_NPG]
    lf_hbm, o_ref, q_b, acc, m_s, l_s, r_s, lf_buf, lf_sem = refs[FOX_NPG:]
    n_pages = r_s.shape[0]
    b, j = pl.program_id(0), pl.program_id(1)

    def lf_copy(b_, p, slot_):
        return pltpu.make_async_copy(lf_hbm.at[pt_ref[b_, p]], lf_buf.at[slot_, p], lf_sem.at[slot_])

    @pl.when(j == 0)
    def _():
        slot = b % 2

        @pl.when(b == 0)
        def _():
            for p in range(n_pages):
                lf_copy(b, p, slot).start()

        @pl.when(b + 1 < pl.num_programs(0))
        def _():
            for p in range(n_pages):
                lf_copy(b + 1, p, 1 - slot).start()

        for h in range(FOX_H):
            q_b[h] = jnp.broadcast_to(_row_to_col(_hrow(hs_ref, "fq", h * HD, HD) * SCALE), (HD, PAGE_SIZE))
        acc[...] = jnp.zeros(acc.shape, F32)
        m_s[...] = jnp.full(m_s.shape, NEG, F32)
        l_s[...] = jnp.zeros(l_s.shape, F32)
        for p in range(n_pages):
            lf_copy(0, p, slot).wait()
        incl = _lane_cumsum(lf_buf[slot].reshape(n_pages * FOX_H, PAGE_SIZE))
        later = _row_to_col(_hrow(hs_ref, "flog"))
        for p in reversed(range(n_pages)):
            inc = incl[p * FOX_H:(p + 1) * FOX_H]
            total = inc[:, PAGE_SIZE - 1:PAGE_SIZE]
            r_s[p] = later + (total - inc)
            later = later + total

    s_list = []
    for i in range(FOX_NPG):
        qk = jnp.concatenate([jnp.sum(kv_refs[i][0, 0, h] * q_b[h], axis=0, keepdims=True) for h in range(FOX_H)], axis=0)
        s_list.append(qk + r_s[j * FOX_NPG + i])
    s_max = s_list[0]
    for s in s_list[1:]:
        s_max = jnp.maximum(s_max, s)
    m_old = m_s[...]
    m_new = jnp.maximum(m_old, jnp.max(s_max, axis=1, keepdims=True))
    alpha = jnp.exp(m_old - m_new)
    p_list = [jnp.exp(s - m_new) for s in s_list]
    p_sum = p_list[0]
    for p in p_list[1:]:
        p_sum = p_sum + p
    m_s[...] = m_new
    l_s[...] = l_s[...] * alpha + jnp.sum(p_sum, axis=1, keepdims=True)
    for h in range(FOX_H):
        a = acc[h] * alpha[h:h + 1, :]
        for i in range(FOX_NPG):
            a = a + kv_refs[i][0, 1, h] * p_list[i][h:h + 1, :]
        acc[h] = a

    @pl.when(j == pl.num_programs(1) - 1)
    def _():
        outs = []
        for h in range(FOX_H):
            q = _hrow(hs_ref, "fq", h * HD, HD) * SCALE
            kn, vn = _hrow(hs_ref, "fkv", h * HD, HD), _hrow(hs_ref, "fkv", FOX_H * HD + h * HD, HD)
            s_n = jnp.sum(q * kn, axis=1, keepdims=True)
            m_h, l_h = m_s[h:h + 1, :], l_s[h:h + 1, :]
            m_f = jnp.maximum(m_h, s_n)
            al, pn = jnp.exp(m_h - m_f), jnp.exp(s_n - m_f)
            past = _col_to_row(jnp.sum(acc[h], axis=1, keepdims=True))
            outs.append((past * al + pn * vn) / jnp.maximum(l_h * al + pn, TINY))
        o_ref[0] = jnp.concatenate(outs, axis=1) * _silu(_hrow(hs_ref, "fsilu"))


def _fox_sample(page_table, hs3, pool_kv, pool_lf):
    n, n_pages = page_table.shape
    assert n_pages % FOX_NPG == 0
    ng = n_pages // FOX_NPG
    kv_specs = [pl.BlockSpec((1, 2, FOX_H, HD, PAGE_SIZE), lambda b, j, pt, i=i: (pt[b, j * FOX_NPG + i], 0, 0, 0, 0))
                for i in range(FOX_NPG)]
    grid_spec = pltpu.PrefetchScalarGridSpec(
        num_scalar_prefetch=1, grid=(n, ng),
        in_specs=[pl.BlockSpec((1, 1, N_IN_PAD), lambda b, j, pt: (b, 0, 0))] + kv_specs
                 + [pl.BlockSpec(memory_space=pl.ANY)],
        out_specs=pl.BlockSpec((1, 1, BR_W), lambda b, j, pt: (b, 0, 0)),
        scratch_shapes=[pltpu.VMEM((FOX_H, HD, PAGE_SIZE), F32), pltpu.VMEM((FOX_H, HD, PAGE_SIZE), F32),
                        pltpu.VMEM((FOX_H, 1), F32), pltpu.VMEM((FOX_H, 1), F32),
                        pltpu.VMEM((n_pages, FOX_H, PAGE_SIZE), F32),
                        pltpu.VMEM((2, n_pages, FOX_H, PAGE_SIZE), F32), pltpu.SemaphoreType.DMA((2,))],
    )
    return pl.pallas_call(
        _fox_sample_kernel, grid_spec=grid_spec,
        out_shape=jax.ShapeDtypeStruct((n, 1, BR_W), F32),
        compiler_params=pltpu.CompilerParams(dimension_semantics=("arbitrary", "arbitrary"),
                                             vmem_limit_bytes=VMEM_LIMIT),
        name="fox_sample",
    )(page_table, hs3, *([pool_kv] * FOX_NPG), pool_lf)


def _split_weights(w_in):
    wt = w_in.T
    sl = lambda k: wt[_COLS[k][0]:_COLS[k][1]]
    d = wt.shape[1]
    wq = jnp.concatenate([sl(k) for k in _QGROUPS] + [sl("bgate"), sl("ngate"),
                                                        jnp.zeros((LANES - 24, d), wt.dtype)], axis=0).astype(BF16)
    wkv = jnp.concatenate([sl(k) for k, _ in _KVGROUPS], axis=0).astype(BF16)
    w_all = jnp.concatenate([sl(k) for k in _S_ORDER] + [jnp.zeros((N_IN_PAD - N_IN, d), wt.dtype)],
                            axis=0).astype(BF16)
    return wq, wkv, w_all


def kernel(x_prompt, x_sample, cache_nsa_cmp, cache_nsa_slc, state_nsa_win, cache_fox_kv, cache_fox_logf, cache_mem_kv, page_table, mem_prompt, g_pre, g_post, g_mem, w_in, b_fgate, w_mem_kv, w_cmp1, w_cmp2, pe_cmp, w_branch, w_out):
    assert w_in.shape[0] == 1 and x_sample.shape[1] == 1, "one layer, one new token per sequence"
    b, t, d = x_prompt.shape
    n_s, n_pages = page_table.shape
    past = n_pages * PAGE_SIZE
    wq, wkv, w_all = _split_weights(w_in[0])
    w1bd, w2bd, pe_t = _compress_weights(w_cmp1[0], w_cmp2[0], pe_cmp[0])
    wb, wo = w_branch[0].astype(BF16), w_out[0].astype(BF16)

    (nq, nsilu, fq, fsilu, mq, msilu, bgate, ngate, cmp_t, slc_t, win_t, fkv_t, flog_t) = _proj_prompt(
        x_prompt, g_pre, wq, wkv, _rope_tables(jnp.arange(t, dtype=F32)), b_fgate[0])
    cb_p = _compress(cmp_t.reshape(b, 2, NSA_KV, HD, t), jnp.zeros((1, 1), jnp.int32), w1bd, w2bd, pe_t,
                     n_batch=b, n_pages=t // PAGE_SIZE, paged=False)
    cb_p = cb_p.reshape(b, 2, NSA_KV, t // CMP_BLOCK, HD)
    a_nsa = _nsa_prompt(nq, slc_t, win_t, cb_p, cb_p, ngate, nsilu)
    a_fox = _fox_prompt(fq, fkv_t, flog_t, fsilu)
    mkv_p = _mem_kv(mem_prompt, g_mem, w_mem_kv[0].astype(BF16))
    a_mem = _mem_attn(mq, msilu, mkv_p)
    y_p = _merge(a_nsa, a_fox, a_mem, bgate, x_prompt, wb, wo, g_post)

    rows_minor = lambda a: jnp.transpose(a, (0, 2, 3, 4, 1))
    hs = _proj_sample(x_sample[:, 0, :], g_pre, w_all, _rope_tables(jnp.full((n_s,), past, F32)), b_fgate[0])
    hs3 = hs.reshape(n_s, 1, N_IN_PAD)
    cb_s = _compress(rows_minor(cache_nsa_cmp[0]), page_table, w1bd, w2bd, pe_t,
                     n_batch=n_s, n_pages=n_pages, paged=True)
    cb_s = cb_s.reshape(n_s, 2, NSA_KV, past // CMP_BLOCK, HD)
    o_c, idx = _nsa_sample_sel(hs3, cb_s)
    a_nsa_s, new_win = _nsa_sample_attn(idx.reshape(n_s, NSA_KV * LANES), page_table, hs3, o_c,
                                        rows_minor(state_nsa_win[0]), rows_minor(cache_nsa_slc[0]))
    a_fox_s = _fox_sample(page_table, hs3, rows_minor(cache_fox_kv[0]),
                          jnp.transpose(cache_fox_logf[0], (0, 2, 1)))
    col = lambda k: hs[:, _S_OFF[k]:_S_OFF[k] + _COLS[k][1] - _COLS[k][0]]
    a_mem_s = _mem_attn(col("mq").reshape(n_s, 1, BR_W), col("msilu").reshape(n_s, 1, BR_W),
                        cache_mem_kv[0].reshape(n_s, -1, 2 * BR_W))
    y_s = _merge(a_nsa_s.reshape(1, n_s, BR_W), a_fox_s.reshape(1, n_s, BR_W), a_mem_s.reshape(1, n_s, BR_W),
                 col("bgate").reshape(1, n_s, 3 * d), x_sample.reshape(1, n_s, d), wb, wo, g_post)

    def rows_major(a_t, heads):
        n, _, r = a_t.shape
        return jnp.transpose(a_t.reshape(n, 2, heads, HD, r), (0, 4, 1, 2, 3))[None]

    wkeep = min(WINDOW, t)
    return (y_p, y_s.reshape(n_s, 1, d),
            rows_major(cmp_t, NSA_KV), rows_major(slc_t, NSA_KV), rows_major(win_t[:, :, t - wkeep:], NSA_KV),
            rows_major(fkv_t, FOX_H), jnp.transpose(flog_t, (0, 2, 1))[None],
            mkv_p.reshape(1, b, -1, 2, MEM_H, MEM_HD),
            col("ncmp").reshape(1, n_s, 1, 2, NSA_KV, HD), col("nslc").reshape(1, n_s, 1, 2, NSA_KV, HD),
            jnp.transpose(new_win, (0, 4, 1, 2, 3))[None],
            col("fkv").reshape(1, n_s, 1, 2, FOX_H, HD), col("flog").reshape(1, n_s, 1, FOX_H))
```

```python
import functools

import numpy as np
import jax
import jax.numpy as jnp
from jax import lax
from jax.experimental import pallas as pl
from jax.experimental.pallas import tpu as pltpu

HD = 64
NSA_H = 8
NSA_KV = 2
NSA_R = NSA_H // NSA_KV
CMP_BLOCK = 64
N_SELECT = 16
WINDOW = 512
FOX_H = 8
MEM_H = 4
MEM_HD = 128
BR_W = 512
ROT_DIM = HD // 4
ROT_HALF = ROT_DIM // 2
ROPE_THETA = 500000.0
PAGE_SIZE = 128
EPS = 1e-6
NEG = -1e30
BIG = 1e4
TINY = 1e-30
KV_W = NSA_KV * HD
SCALE = HD ** -0.5

LANES = 128
SUBLANES = 8
VMEM_LIMIT = 56 * 1024 * 1024

_COLS = dict(nq=(0, 512), ncmp=(512, 768), nslc=(768, 1024), nwin=(1024, 1280), ngate=(1280, 1304),
             nsilu=(1304, 1816), fq=(1816, 2328), fkv=(2328, 3352), flog=(3352, 3360), fsilu=(3360, 3872),
             mq=(3872, 4384), msilu=(4384, 4896), bgate=(4896, 7968))
N_IN = 7968
N_IN_PAD = 8064

F32 = jnp.float32
BF16 = jnp.bfloat16


def _nt_dot(a, b):
    return lax.dot_general(a, b, (((1,), (1,)), ((), ())), preferred_element_type=F32)


def _nn_dot(a, b):
    return lax.dot_general(a, b, (((1,), (0,)), ((), ())), preferred_element_type=F32)


def _rmsnorm_rows(x, g):
    var = jnp.mean(x * x, axis=-1, keepdims=True)
    return x * lax.rsqrt(var + EPS) * g


def _silu(x):
    return x * (1.0 / (1.0 + jnp.exp(-x)))


def _sigmoid(x):
    return 1.0 / (1.0 + jnp.exp(-x))


def _rope_tables(pos):
    inv_freq = ROPE_THETA ** (-jnp.arange(ROT_HALF, dtype=F32) / ROT_HALF)
    ang = pos[:, None] * inv_freq[None, :]
    cos, sin = jnp.cos(ang), jnp.sin(ang)
    t = pos.shape[0]
    one = jnp.ones((t, HD - ROT_DIM), F32)
    zero = jnp.zeros((t, HD - ROT_HALF), F32)
    c_head = jnp.concatenate([cos, cos, one], axis=1)
    s_hi = jnp.concatenate([-sin, zero], axis=1)
    s_lo = jnp.concatenate([jnp.zeros((t, ROT_HALF), F32), sin, jnp.zeros((t, HD - ROT_DIM), F32)], axis=1)
    tile2 = lambda a: jnp.concatenate([a, a], axis=1)
    return tile2(c_head), tile2(s_hi), tile2(s_lo), cos, sin


def _rope_lanes(x, c, s_hi, s_lo):
    k = x.shape[1] // LANES
    if k > 1:
        c = jnp.concatenate([c] * k, axis=1)
        s_hi = jnp.concatenate([s_hi] * k, axis=1)
        s_lo = jnp.concatenate([s_lo] * k, axis=1)
    n = x.shape[1]
    up = pltpu.roll(x, n - ROT_HALF, axis=1)
    dn = pltpu.roll(x, ROT_HALF, axis=1)
    return x * c + up * s_hi + dn * s_lo


_QGROUPS = ("nq", "nsilu", "fq", "fsilu", "mq", "msilu")
_KVGROUPS = (("ncmp", 256), ("nslc", 256), ("nwin", 256), ("fkv", 1024), ("flog", 8))
NQ_ROWS = 6 * 512 + 3072 + LANES
NKV_ROWS = 256 * 3 + 1024 + 8
PROJ_TM = 512


def _proj_prompt_kernel(x_ref, g_ref, wq_ref, wkv_ref, c_ref, shi_ref, slo_ref, cosT_ref, sinT_ref, bf_ref,
                        nq_ref, nsilu_ref, fq_ref, fsilu_ref, mq_ref, msilu_ref, bgate_ref, ngate_ref,
                        cmpT_ref, slcT_ref, winT_ref, fkvT_ref, flogT_ref):
    xb = _rmsnorm_rows(x_ref[0], g_ref[...]).astype(BF16)
    outs = (nq_ref, nsilu_ref, fq_ref, fsilu_ref, mq_ref, msilu_ref)
    for i, o_ref in enumerate(outs):
        h = _nt_dot(xb, wq_ref[i * 512:(i + 1) * 512, :])
        if i == 0:
            h = _rope_lanes(h, c_ref[...], shi_ref[...], slo_ref[...])
        if i in (0, 2):
            h = h * SCALE
        o_ref[0] = h.astype(o_ref.dtype)
    off = 6 * 512
    for j in range(3072 // 512):
        h = _nt_dot(xb, wq_ref[off + j * 512: off + (j + 1) * 512, :])
        bgate_ref[0, :, j * 512:(j + 1) * 512] = h.astype(bgate_ref.dtype)
    off += 3072
    ngate_ref[0] = _nt_dot(xb, wq_ref[off:off + LANES, :])
    cosT, sinT = cosT_ref[...], sinT_ref[...]
    roff = 0
    for o_ref in (cmpT_ref, slcT_ref, winT_ref):
        hT = _nt_dot(wkv_ref[roff:roff + 256, :], xb)
        o_ref[0] = hT
        for g in range(NSA_KV):
            x1 = hT[g * HD: g * HD + ROT_HALF, :]
            x2 = hT[g * HD + ROT_HALF: g * HD + ROT_DIM, :]
            o_ref[0, g * HD: g * HD + ROT_HALF, :] = x1 * cosT - x2 * sinT
            o_ref[0, g * HD + ROT_HALF: g * HD + ROT_DIM, :] = x2 * cosT + x1 * sinT
        roff += 256
    for j in range(1024 // 256):
        fkvT_ref[0, j * 256:(j + 1) * 256, :] = _nt_dot(wkv_ref[roff + j * 256: roff + (j + 1) * 256, :], xb)
    roff += 1024
    flogT_ref[0] = _log_sigmoid(_nt_dot(wkv_ref[roff:roff + 8, :], xb) + bf_ref[...])


def _proj_prompt(x, g_pre, wq, wkv, tabs, b_fgate):
    b, t, d = x.shape
    tm = min(PROJ_TM, t)
    nt = t // tm
    c, shi, slo, cos, sin = tabs
    cosT, sinT = cos.T, sin.T
    row = lambda bi, ti: (bi, ti, 0)
    col = lambda bi, ti: (bi, 0, ti)
    const = lambda bi, ti: (0, 0)
    tab = lambda bi, ti: (ti, 0)
    tabT = lambda bi, ti: (0, ti)
    resident = dict(pipeline_mode=pl.Buffered(1))
    in_specs = [
        pl.BlockSpec((1, tm, d), row),
        pl.BlockSpec((1, d), const),
        pl.BlockSpec((NQ_ROWS, d), const, **resident),
        pl.BlockSpec((NKV_ROWS, d), const, **resident),
        pl.BlockSpec((tm, LANES), tab), pl.BlockSpec((tm, LANES), tab), pl.BlockSpec((tm, LANES), tab),
        pl.BlockSpec((ROT_HALF, tm), tabT), pl.BlockSpec((ROT_HALF, tm), tabT),
        pl.BlockSpec((FOX_H, 1), const),
    ]
    q_shape = jax.ShapeDtypeStruct((b, t, 512), BF16)
    out_shape = [q_shape] * 6 + [
        jax.ShapeDtypeStruct((b, t, 3072), BF16),
        jax.ShapeDtypeStruct((b, t, LANES), F32),
        jax.ShapeDtypeStruct((b, 256, t), F32), jax.ShapeDtypeStruct((b, 256, t), F32),
        jax.ShapeDtypeStruct((b, 256, t), F32),
        jax.ShapeDtypeStruct((b, 1024, t), F32),
        jax.ShapeDtypeStruct((b, 8, t), F32),
    ]
    out_specs = [pl.BlockSpec((1, tm, 512), row)] * 6 + [
        pl.BlockSpec((1, tm, 3072), row),
        pl.BlockSpec((1, tm, LANES), row),
        pl.BlockSpec((1, 256, tm), col), pl.BlockSpec((1, 256, tm), col), pl.BlockSpec((1, 256, tm), col),
        pl.BlockSpec((1, 1024, tm), col),
        pl.BlockSpec((1, 8, tm), col),
    ]
    return pl.pallas_call(
        _proj_prompt_kernel,
        grid=(b, nt),
        in_specs=in_specs,
        out_specs=out_specs,
        out_shape=out_shape,
        compiler_params=pltpu.CompilerParams(dimension_semantics=("arbitrary", "arbitrary"),
                                             vmem_limit_bytes=VMEM_LIMIT),
        name="proj_prompt",
    )(x, g_pre, wq, wkv, c, shi, slo, cosT, sinT, b_fgate.reshape(FOX_H, 1))


PROJ_S_TN = 1152


def _proj_sample_kernel(x_ref, g_ref, w_ref, c_ref, shi_ref, slo_ref, bf_ref, o_ref):
    xb = _rmsnorm_rows(x_ref[...], g_ref[...]).astype(BF16)
    o_ref[...] = _nt_dot(xb, w_ref[...])

    @pl.when(pl.program_id(0) == _S_OFF["flog"] // PROJ_S_TN)
    def _():
        lo = _S_OFF["flog"] % PROJ_S_TN
        o_ref[:, lo:lo + FOX_H] = _log_sigmoid(o_ref[:, lo:lo + FOX_H] + bf_ref[...])

    @pl.when(pl.program_id(0) == 0)
    def _():
        c, shi, slo = c_ref[...], shi_ref[...], slo_ref[...]
        for c0 in (0, 128, 256, 384, 512, 768, 1024):
            o_ref[:, c0:c0 + LANES] = _rope_lanes(o_ref[:, c0:c0 + LANES], c, shi, slo)


def _proj_sample(x, g_pre, w_all, tabs, b_fgate):
    n, d = x.shape
    c, shi, slo, _, _ = tabs
    const = lambda j: (0, 0)
    return pl.pallas_call(
        _proj_sample_kernel,
        grid=(N_IN_PAD // PROJ_S_TN,),
        in_specs=[pl.BlockSpec((n, d), const), pl.BlockSpec((1, d), const),
                  pl.BlockSpec((PROJ_S_TN, d), lambda j: (j, 0)),
                  pl.BlockSpec((n, LANES), const), pl.BlockSpec((n, LANES), const), pl.BlockSpec((n, LANES), const),
                  pl.BlockSpec((1, FOX_H), const)],
        out_specs=pl.BlockSpec((n, PROJ_S_TN), lambda j: (0, j)),
        out_shape=jax.ShapeDtypeStruct((n, N_IN_PAD), F32),
        compiler_params=pltpu.CompilerParams(dimension_semantics=("arbitrary",)),
        name="proj_sample",
    )(x, g_pre, w_all, c, shi, slo, b_fgate.reshape(1, FOX_H))


CMP_PITCH = 72
CMP_DCHUNK = 8


def _compress_kernel(pt_ref, src_ref, w1_ref, w2_ref, pe_ref, o_ref, buf, sem, *, n_pages, paged):
    bi, kv = pl.program_id(0), pl.program_id(1)
    n_steps = pl.num_programs(0) * 2
    step = bi * 2 + kv
    slot = step % 2

    def page_copy(b_, kv_, i, slot_):
        if paged:
            src = src_ref.at[pt_ref[b_, i], kv_]
        else:
            src = src_ref.at[b_, kv_, :, :, pl.ds(i * PAGE_SIZE, PAGE_SIZE)]
        dst = buf.at[slot_, :, pl.ds(i * CMP_PITCH, HD), :]
        return pltpu.make_async_copy(src, dst, sem.at[slot_])

    def issue(b_, kv_, slot_):
        for i in range(n_pages):
            page_copy(b_, kv_, i, slot_).start()

    @pl.when(step == 0)
    def _():
        issue(bi, kv, slot)

    @pl.when(step + 1 < n_steps)
    def _():
        nxt = step + 1
        issue(nxt // 2, nxt % 2, 1 - slot)

    for i in range(n_pages):
        page_copy(0, 0, i, slot).wait()

    acc = jnp.zeros((2 * n_pages, LANES), F32)
    for c in range(HD // CMP_DCHUNK):
        cols = []
        for dd in range(CMP_DCHUNK):
            d = c * CMP_DCHUNK + dd
            rows = [buf[slot, g, pl.ds(d, n_pages, stride=CMP_PITCH), :] for g in range(NSA_KV)]
            x = jnp.concatenate(rows, axis=0) + pe_ref[0, d:d + 1, :]
            cols.append(x.astype(BF16))
        lhs = jnp.concatenate(cols, axis=1)
        k0 = c * CMP_DCHUNK * LANES
        acc = acc + _nn_dot(lhs, w1_ref[0, k0:k0 + CMP_DCHUNK * LANES, :])
    hid = _silu(acc)
    out = _nn_dot(hid.astype(BF16), w2_ref[0])
    o_ref[0, 0, 0] = out[:n_pages]
    o_ref[0, 0, 1] = out[n_pages:]


def _compress(src, page_table, w1bd, w2bd, pe_t, *, n_batch, n_pages, paged):
    kern = functools.partial(_compress_kernel, n_pages=n_pages, paged=paged)
    grid_spec = pltpu.PrefetchScalarGridSpec(
        num_scalar_prefetch=1,
        grid=(n_batch, 2),
        in_specs=[pl.BlockSpec(memory_space=pl.ANY),
                  pl.BlockSpec((1, HD * LANES, LANES), lambda b, kv, pt: (kv, 0, 0)),
                  pl.BlockSpec((1, LANES, LANES), lambda b, kv, pt: (kv, 0, 0)),
                  pl.BlockSpec((1, HD, LANES), lambda b, kv, pt: (kv, 0, 0))],
        out_specs=pl.BlockSpec((1, 1, NSA_KV, n_pages, LANES), lambda b, kv, pt: (b, kv, 0, 0, 0)),
        scratch_shapes=[pltpu.VMEM((2, NSA_KV, n_pages * CMP_PITCH, LANES), F32),
                        pltpu.SemaphoreType.DMA((2,))],
    )
    return pl.pallas_call(
        kern, grid_spec=grid_spec,
        out_shape=jax.ShapeDtypeStruct((n_batch, 2, NSA_KV, n_pages, LANES), F32),
        compiler_params=pltpu.CompilerParams(dimension_semantics=("arbitrary", "arbitrary"),
                                             vmem_limit_bytes=VMEM_LIMIT),
        name="compress_paged" if paged else "compress_prompt",
    )(page_table, src, w1bd, w2bd, pe_t)


def _compress_weights(w1, w2, pe):
    eye2 = jnp.eye(2, dtype=w1.dtype)
    base = jnp.transpose(w1, (0, 2, 1, 3)).astype(BF16)
    zero = jnp.zeros_like(base)
    w1bd = jnp.stack([jnp.concatenate([base, zero], axis=-1), jnp.concatenate([zero, base], axis=-1)], axis=2)
    w1bd = w1bd.reshape(2, HD * LANES, LANES)
    w2bd = jnp.einsum('kef,nm->knemf', w2, eye2).reshape(2, LANES, LANES).astype(BF16)
    pe_t = jnp.tile(jnp.transpose(pe, (0, 2, 1)), (1, 1, 2))
    return w1bd, w2bd, pe_t


def _log_sigmoid(x):
    return jnp.minimum(x, 0.0) - jnp.log1p(jnp.exp(-jnp.abs(x)))


def _lane_cumsum(x):
    lane = lax.broadcasted_iota(jnp.int32, x.shape, 1)
    s = 1
    while s < LANES:
        x = x + jnp.where(lane >= s, pltpu.roll(x, s, axis=1), 0.0)
        s *= 2
    return x


M_INIT = -1e29


def _online_update(carry, s, v_t):
    m, l, acc = carry
    m_new = jnp.maximum(m, jnp.max(s, axis=0, keepdims=True))
    alpha = jnp.exp(m - m_new)
    p = jnp.exp(s - m_new)
    l = l * alpha + jnp.sum(p, axis=0, keepdims=True)
    pb = p.astype(BF16)
    if isinstance(v_t, (list, tuple)):
        w = pb.shape[1] // len(v_t)
        pv = jnp.concatenate([_nn_dot(v, pb[:, j * w:(j + 1) * w]) for j, v in enumerate(v_t)], axis=1)
    else:
        pv = _nn_dot(v_t, pb)
    return m_new, l, acc * alpha + pv


def _online_init(nq, d):
    return (jnp.full((1, nq), M_INIT, F32), jnp.zeros((1, nq), F32), jnp.zeros((d, nq), F32))


def _mask_bias(cond):
    return jnp.where(cond, 0.0, NEG)


def _online_finish(carry):
    _, l, acc = carry
    return acc / jnp.maximum(l, TINY)


ATT_T = 256
FOX_HPS = 4


def _fox_prompt_kernel(q_ref, k_ref, v_ref, logf_ref, gate_ref, o_ref, k_r, v_t, v_t2, c_b):
    hg, qi = pl.program_id(1), pl.program_id(2)
    tk = ATT_T
    tq = ATT_T
    t = k_ref.shape[2]
    nkt = t // tk
    nh = FOX_HPS

    @pl.when(qi == 0)
    def _():
        for k2 in range(nkt // 2):
            sl = slice(2 * k2 * tk, 2 * (k2 + 1) * tk)
            k_r[k2] = k_ref[0, :, sl].T.astype(BF16)
            v = v_ref[0, :, sl].astype(BF16)
            v_t2[k2] = v
            v_t[2 * k2] = v[:, :tk]
            v_t[2 * k2 + 1] = v[:, tk:]
        carry = jnp.zeros((FOX_H, 1), F32)
        per = 2 * tk // LANES
        for ch in range(t // LANES):
            x = _lane_cumsum(logf_ref[0, :, ch * LANES:(ch + 1) * LANES]) + carry
            carry = x[:, LANES - 1:LANES]
            head = lax.broadcasted_iota(jnp.int32, x.shape, 0)
            for j in range(nh):
                xr = jnp.sum(jnp.where(head == nh * hg + j, x, 0.0), axis=0, keepdims=True)
                c_b[j, ch // per, (ch % per) * LANES:(ch % per + 1) * LANES, :] = jnp.broadcast_to(xr, (LANES, LANES)).T

    q_t = q_ref[0].astype(F32).T.astype(BF16)
    row_head = lax.broadcasted_iota(jnp.int32, q_t.shape, 0) // HD
    q_bd = jnp.concatenate([jnp.where(row_head == j, q_t, jnp.zeros_like(q_t)) for j in range(nh)], axis=1)
    def rows(kt):
        return pl.ds(pl.multiple_of((kt & 1) * tk, tk), tk)

    c_0 = [c_b[j, qi >> 1, rows(qi), :][0:1, :] for j in range(nh)]

    def logits(keys, c_keys):
        bias = jnp.concatenate([c_keys(j) - c_0[j] for j in range(nh) for _ in range(tq // LANES)], axis=1)
        return _nn_dot(keys, q_bd) - bias

    def single(kt):
        return logits(k_r[kt >> 1, rows(kt), :], lambda j: c_b[j, kt >> 1, rows(kt), :])

    def values(v):
        return [v[j * HD:(j + 1) * HD, :] for j in range(nh)]

    def body(k2, carry):
        return _online_update(carry, logits(k_r[k2], lambda j: c_b[j, k2]), values(v_t2[k2]))

    carry = lax.fori_loop(0, qi >> 1, body, _online_init(nh * tq, HD))
    carry = lax.cond((qi & 1) == 1, lambda c: _online_update(c, single(qi - 1), values(v_t[qi - 1])),
                     lambda c: c, carry)
    causal = _mask_bias(lax.broadcasted_iota(jnp.int32, (tk, tq), 0) <= lax.broadcasted_iota(jnp.int32, (tk, tq), 1))
    o_t = _online_finish(_online_update(carry, single(qi) + jnp.concatenate([causal] * nh, axis=1), values(v_t[qi])))
    o = jnp.concatenate([o_t[:, j * tq:(j + 1) * tq] for j in range(nh)], axis=0).T
    o_ref[0] = (o * _silu(gate_ref[0].astype(F32))).astype(o_ref.dtype)


def _fox_prompt(fq, fkv_t, logf_t, fsilu):
    b, t, _ = fq.shape
    tq = ATT_T
    nkt = t // tq
    ng = FOX_H // FOX_HPS
    gw = FOX_HPS * HD
    return pl.pallas_call(
        _fox_prompt_kernel,
        grid=(b, ng, t // tq),
        in_specs=[pl.BlockSpec((1, tq, gw), lambda bi, hg, qi: (bi, qi, hg)),
                  pl.BlockSpec((1, gw, t), lambda bi, hg, qi: (bi, hg, 0)),
                  pl.BlockSpec((1, gw, t), lambda bi, hg, qi: (bi, ng + hg, 0)),
                  pl.BlockSpec((1, FOX_H, t), lambda bi, hg, qi: (bi, 0, 0)),
                  pl.BlockSpec((1, tq, gw), lambda bi, hg, qi: (bi, qi, hg))],
        out_specs=pl.BlockSpec((1, tq, gw), lambda bi, hg, qi: (bi, qi, hg)),
        out_shape=jax.ShapeDtypeStruct((b, t, BR_W), BF16),
        scratch_shapes=[pltpu.VMEM((nkt // 2, 2 * tq, gw), BF16), pltpu.VMEM((nkt, gw, tq), BF16),
                        pltpu.VMEM((nkt // 2, gw, 2 * tq), BF16),
                        pltpu.VMEM((FOX_HPS, nkt // 2, 2 * tq, LANES), F32)],
        compiler_params=pltpu.CompilerParams(dimension_semantics=("arbitrary",) * 3, vmem_limit_bytes=VMEM_LIMIT),
        name="fox_prompt",
    )(fq, fkv_t, fkv_t, logf_t, fsilu)


def _nsa_prompt_kernel(q_ref, ks_ref, vs_ref, kw_ref, vw_ref, kc_ref, vc_ref, ngate_ref, gate_ref, o_ref,
                       ks_r, vs_t, vs_t2, kw_r, vw_t):
    g, qi = pl.program_id(1), pl.program_id(2)
    tk = ATT_T
    tq = ATT_T
    t = ks_ref.shape[2]
    nkt = t // tk
    nb = t // CMP_BLOCK
    nq = NSA_R * tq

    @pl.when(qi == 0)
    def _():
        for k2 in range(nkt // 2):
            sl = slice(2 * k2 * tk, 2 * (k2 + 1) * tk)
            k_rows = jnp.concatenate([ks_ref[0, :, sl], kw_ref[0, :, sl]], axis=0).T
            ks_r[k2] = k_rows[:, :HD].astype(BF16)
            vs = vs_ref[0, :, sl].astype(BF16)
            vw = vw_ref[0, :, sl].astype(BF16)
            vs_t2[k2] = vs
            for half in range(2):
                kt = 2 * k2 + half
                kw_r[kt] = k_rows[half * tk:(half + 1) * tk, HD:].astype(BF16)
                vs_t[kt] = vs[:, half * tk:(half + 1) * tk]
                vw_t[kt] = vw[:, half * tk:(half + 1) * tk]

    q_t = q_ref[0].astype(F32).T
    q4 = jnp.concatenate([q_t[r * HD:(r + 1) * HD] for r in range(NSA_R)], axis=1).astype(BF16)
    q0 = qi * tq
    tpos4 = q0 + lax.broadcasted_iota(jnp.int32, (nb, nq), 1) % tq
    blk4 = lax.broadcasted_iota(jnp.int32, (nb, nq), 0)
    readable = (blk4 + 1) * CMP_BLOCK - 1 <= tpos4
    s_c = jnp.where(readable, _nn_dot(kc_ref[0, 0, 0].astype(BF16), q4), NEG)
    m_c = jnp.max(s_c, axis=0, keepdims=True)
    e_c = jnp.where(readable, jnp.exp(s_c - m_c), 0.0)
    p_c = e_c / jnp.maximum(jnp.sum(e_c, axis=0, keepdims=True), TINY)
    eye = jnp.where(lax.broadcasted_iota(jnp.int32, (HD, HD), 0) == lax.broadcasted_iota(jnp.int32, (HD, HD), 1),
                    1.0, 0.0).astype(BF16)
    vc_t = _nt_dot(eye, vc_ref[0, 0, 0].astype(BF16)).astype(BF16)
    o_c = _nn_dot(vc_t, p_c.astype(BF16))
    imp = p_c[:, 0:tq] + p_c[:, tq:2 * tq] + p_c[:, 2 * tq:3 * tq] + p_c[:, 3 * tq:4 * tq]
    tpos = q0 + lax.broadcasted_iota(jnp.int32, (nb, tq), 1)
    blk = lax.broadcasted_iota(jnp.int32, (nb, tq), 0)
    cur = tpos // CMP_BLOCK
    score = jnp.where(blk < cur, imp, jnp.where(blk == cur, BIG, NEG))
    rank = jnp.zeros((nb, tq), F32)
    for mth in range(nb):
        rowv = score[mth:mth + 1, :]
        ahead = (rowv > score) | ((rowv == score) & (blk > mth))
        rank = rank + jnp.where(ahead, 1.0, 0.0)
    sel_bias = _mask_bias((rank < N_SELECT) & (score > 0.5 * NEG)).astype(BF16)

    key = lax.broadcasted_iota(jnp.int32, (tk, tq), 0)
    qry = lax.broadcasted_iota(jnp.int32, (tk, tq), 1)
    heads = lambda a: jnp.concatenate([a] * NSA_R, axis=1)

    def sel_block_bias(k0, n):
        kb = (k0 + lax.broadcasted_iota(jnp.int32, (n, nb), 0)) // CMP_BLOCK
        expand = jnp.where(kb == lax.broadcasted_iota(jnp.int32, (n, nb), 1), 1.0, 0.0).astype(BF16)
        return _nn_dot(expand, sel_bias)

    def sel_keys(kt):
        return ks_r[kt >> 1, pl.ds(pl.multiple_of((kt & 1) * tk, tk), tk), :]

    def sel_body(k2, carry):
        s = _nn_dot(ks_r[k2], q4) + heads(sel_block_bias(k2 * 2 * tk, 2 * tk))
        return _online_update(carry, s, vs_t2[k2])

    def sel_single(carry):
        kt = qi - 1
        s = _nn_dot(sel_keys(kt), q4) + heads(sel_block_bias(kt * tk, tk))
        return _online_update(carry, s, vs_t[kt])

    carry = lax.fori_loop(0, qi >> 1, sel_body, _online_init(nq, HD))
    carry = lax.cond((qi & 1) == 1, sel_single, lambda c: c, carry)
    s, bias = _nn_dot(sel_keys(qi), q4), sel_block_bias(qi * tk, tk)
    carry = _online_update(carry, s + heads(bias + _mask_bias(key <= qry)), vs_t[qi])
    o_s = _online_finish(carry)
    carry = _online_init(nq, HD)
    k2 = jnp.maximum(qi - 2, 0)
    carry = _online_update(carry, _nn_dot(kw_r[k2], q4) + heads(_mask_bias((key >= qry) & (qi >= 2))), vw_t[k2])
    k1 = jnp.maximum(qi - 1, 0)
    carry = _online_update(carry, _nn_dot(kw_r[k1], q4) + _mask_bias(qi >= 1), vw_t[k1])
    carry = _online_update(carry, _nn_dot(kw_r[qi], q4) + heads(_mask_bias(key <= qry)), vw_t[qi])
    o_w = _online_finish(carry)
    gts = _sigmoid(ngate_ref[0]).T

    def gate_row(br, r):
        c0 = br * NSA_H + r
        return jnp.where(g == 0, gts[c0:c0 + 1, :], gts[c0 + NSA_R:c0 + NSA_R + 1, :])

    outs = []
    for r in range(NSA_R):
        sl = slice(r * tq, (r + 1) * tq)
        outs.append(gate_row(0, r) * o_c[:, sl] + gate_row(1, r) * o_s[:, sl] + gate_row(2, r) * o_w[:, sl])
    o = jnp.concatenate(outs, axis=0).T * _silu(gate_ref[0].astype(F32))
    o_ref[0] = o.astype(o_ref.dtype)


def _nsa_prompt(nq, slc_t, win_t, kcb, vcb, ngate, nsilu):
    b, t, _ = nq.shape
    tq = ATT_T
    nkt = t // tq
    nb = t // CMP_BLOCK
    gw = NSA_R * HD
    kv_spec = lambda kv: pl.BlockSpec((1, HD, t), lambda bi, g, qi, kv=kv: (bi, kv * NSA_KV + g, 0))
    cb_spec = lambda kv: pl.BlockSpec((1, 1, 1, nb, HD), lambda bi, g, qi, kv=kv: (bi, kv, g, 0, 0))
    return pl.pallas_call(
        _nsa_prompt_kernel,
        grid=(b, NSA_KV, t // tq),
        in_specs=[pl.BlockSpec((1, tq, gw), lambda bi, g, qi: (bi, qi, g)),
                  kv_spec(0), kv_spec(1), kv_spec(0), kv_spec(1),
                  cb_spec(0), cb_spec(1),
                  pl.BlockSpec((1, tq, LANES), lambda bi, g, qi: (bi, qi, 0)),
                  pl.BlockSpec((1, tq, gw), lambda bi, g, qi: (bi, qi, g))],
        out_specs=pl.BlockSpec((1, tq, gw), lambda bi, g, qi: (bi, qi, g)),
        out_shape=jax.ShapeDtypeStruct((b, t, BR_W), BF16),
        scratch_shapes=[pltpu.VMEM((nkt // 2, 2 * tq, HD), BF16), pltpu.VMEM((nkt, HD, tq), BF16),
                        pltpu.VMEM((nkt // 2, HD, 2 * tq), BF16),
                        pltpu.VMEM((nkt, tq, HD), BF16), pltpu.VMEM((nkt, HD, tq), BF16)],
        compiler_params=pltpu.CompilerParams(dimension_semantics=("arbitrary",) * 3, vmem_limit_bytes=VMEM_LIMIT),
        name="nsa_prompt",
    )(nq, slc_t, slc_t, win_t, win_t, kcb, vcb, ngate, nsilu)


MEM_SLOTS = 2 * MEM_H


def _mem_kv_kernel(m_ref, g_ref, w_ref, o_ref):
    n_tok = m_ref.shape[1]
    res = _nn_dot(_rmsnorm_rows(m_ref[0], g_ref[...]).astype(BF16), w_ref[...])
    for c in range(MEM_SLOTS):
        o_ref[0, pl.ds(c, n_tok, stride=MEM_SLOTS), :] = res[:, c * MEM_HD:(c + 1) * MEM_HD]


def _mem_kv(mem, g_mem, w_bf):
    b, m, d = mem.shape
    return pl.pallas_call(
        _mem_kv_kernel, grid=(b,),
        in_specs=[pl.BlockSpec((1, m, d), lambda bi: (bi, 0, 0)), pl.BlockSpec((1, d), lambda bi: (0, 0)),
                  pl.BlockSpec((d, MEM_SLOTS * MEM_HD), lambda bi: (0, 0))],
        out_specs=pl.BlockSpec((1, m * MEM_SLOTS, MEM_HD), lambda bi: (bi, 0, 0)),
        out_shape=jax.ShapeDtypeStruct((b, m * MEM_SLOTS, MEM_HD), F32),
        compiler_params=pltpu.CompilerParams(dimension_semantics=("arbitrary",)),
        name="mem_kv",
    )(mem, g_mem, w_bf)


MEM_SCALE = MEM_HD ** -0.5


def _mem_attn_kernel(mq_ref, ms_ref, mkv_ref, o_ref):
    mq = mq_ref[0].astype(BF16)
    n_tok = mkv_ref.shape[1] // MEM_SLOTS
    outs = []
    for h in range(MEM_H):
        kh = mkv_ref[0, pl.ds(h, n_tok, stride=MEM_SLOTS), :].astype(BF16)
        vh = mkv_ref[0, pl.ds(MEM_H + h, n_tok, stride=MEM_SLOTS), :].astype(BF16)
        s = _nt_dot(mq[:, h * MEM_HD:(h + 1) * MEM_HD], kh) * MEM_SCALE
        e = jnp.exp(s - jnp.max(s, axis=1, keepdims=True))
        p = e / jnp.sum(e, axis=1, keepdims=True)
        outs.append(_nn_dot(p.astype(BF16), vh))
    o_ref[0] = (jnp.concatenate(outs, axis=1) * _silu(ms_ref[0].astype(F32))).astype(o_ref.dtype)


MERGE_TM = 512


def _mem_attn(mq, msilu, mkv):
    b, t, _ = mq.shape
    tm = min(MERGE_TM, t)
    rows = mkv.shape[1]
    row = lambda bi, ti: (bi, ti, 0)
    return pl.pallas_call(
        _mem_attn_kernel, grid=(b, t // tm),
        in_specs=[pl.BlockSpec((1, tm, BR_W), row), pl.BlockSpec((1, tm, BR_W), row),
                  pl.BlockSpec((1, rows, MEM_HD), lambda bi, ti: (bi, 0, 0))],
        out_specs=pl.BlockSpec((1, tm, BR_W), row),
        out_shape=jax.ShapeDtypeStruct((b, t, BR_W), BF16),
        compiler_params=pltpu.CompilerParams(dimension_semantics=("arbitrary", "arbitrary")),
        name="mem_attn",
    )(mq, msilu, mkv)


def _merge_kernel(an_ref, af_ref, am_ref, bg_ref, x_ref, wb_ref, wo_ref, gp_ref, o_ref):
    d = wo_ref.shape[0]
    merged = None
    for k, a_ref in enumerate((an_ref, af_ref, am_ref)):
        z = _nn_dot(a_ref[0].astype(BF16), wb_ref[k])
        term = _sigmoid(bg_ref[0, :, k * d:(k + 1) * d].astype(F32)) * z
        merged = term if merged is None else merged + term
    y = _nn_dot(merged.astype(BF16), wo_ref[...])
    o_ref[0] = x_ref[0] + _rmsnorm_rows(y, gp_ref[...])


def _merge(a_nsa, a_fox, a_mem, bgate, x, wb, wo, g_post):
    b, t, d = x.shape
    tm = min(MERGE_TM, t)
    row = lambda bi, ti: (bi, ti, 0)
    resident = dict(pipeline_mode=pl.Buffered(1))
    return pl.pallas_call(
        _merge_kernel, grid=(b, t // tm),
        in_specs=[pl.BlockSpec((1, tm, BR_W), row), pl.BlockSpec((1, tm, BR_W), row), pl.BlockSpec((1, tm, BR_W), row),
                  pl.BlockSpec((1, tm, 3 * d), row), pl.BlockSpec((1, tm, d), row),
                  pl.BlockSpec((3, BR_W, d), lambda bi, ti: (0, 0, 0), **resident),
                  pl.BlockSpec((d, d), lambda bi, ti: (0, 0), **resident),
                  pl.BlockSpec((1, d), lambda bi, ti: (0, 0))],
        out_specs=pl.BlockSpec((1, tm, d), row),
        out_shape=jax.ShapeDtypeStruct((b, t, d), F32),
        compiler_params=pltpu.CompilerParams(dimension_semantics=("arbitrary", "arbitrary"),
                                             vmem_limit_bytes=VMEM_LIMIT),
        name="merge",
    )(a_nsa, a_fox, a_mem, bgate, x, wb, wo, g_post)


_S_ORDER = ("nq", "ncmp", "nslc", "nwin", "nsilu", "fq", "fkv", "fsilu", "mq", "msilu", "bgate", "ngate", "flog")
_S_OFF = {}
_o = 0
for _k in _S_ORDER:
    _S_OFF[_k] = _o
    _o += _COLS[_k][1] - _COLS[_k][0]
assert _o == N_IN and _S_OFF["ngate"] % LANES == 0
N_PAST_SEL = N_SELECT - 1


def _row_to_col(row):
    n = row.shape[1]
    eye = lax.broadcasted_iota(jnp.int32, (n, n), 0) == lax.broadcasted_iota(jnp.int32, (n, n), 1)
    return jnp.sum(jnp.where(eye, row, 0.0), axis=1, keepdims=True)


def _col_to_row(col):
    n = col.shape[0]
    eye = lax.broadcasted_iota(jnp.int32, (n, n), 0) == lax.broadcasted_iota(jnp.int32, (n, n), 1)
    return jnp.sum(jnp.where(eye, col, 0.0), axis=0, keepdims=True)


def _hrow(hs_ref, key, off=0, n=None):
    lo = _S_OFF[key] + off
    n = (_COLS[key][1] - _COLS[key][0] - off) if n is None else n
    return hs_ref[0, :, lo:lo + n]


def _q_group(hs_ref, key, g):
    return jnp.concatenate([_hrow(hs_ref, key, (g * NSA_R + r) * HD, HD) for r in range(NSA_R)], axis=0) * SCALE


def _nsa_sample_sel_kernel(hs_ref, cb_ref, oc_ref, idx_ref):
    nblk = cb_ref.shape[3]
    idx_ref[...] = jnp.zeros(idx_ref.shape, idx_ref.dtype)
    for g in range(NSA_KV):
        q4 = _q_group(hs_ref, "nq", g).astype(BF16)
        s = _nt_dot(q4, cb_ref[0, 0, g].astype(BF16))
        e = jnp.exp(s - jnp.max(s, axis=1, keepdims=True))
        p = e / jnp.sum(e, axis=1, keepdims=True)
        oc_ref[0, g * NSA_R:(g + 1) * NSA_R, :] = _nn_dot(p.astype(BF16), cb_ref[0, 1, g].astype(BF16))
        imp = jnp.sum(p, axis=0, keepdims=True)
        col = _row_to_col(imp)
        mi = lax.broadcasted_iota(jnp.int32, (nblk, nblk), 0)
        ni = lax.broadcasted_iota(jnp.int32, (nblk, nblk), 1)
        ahead = (col > imp) | ((col == imp) & (mi < ni))
        rank = jnp.sum(jnp.where(ahead, 1.0, 0.0), axis=0, keepdims=True)
        sel = jnp.where(rank < N_PAST_SEL, 1.0, 0.0)
        pos, carry = [], jnp.zeros((1, 1), F32)
        for ch in range(nblk // LANES):
            chunk = sel[:, ch * LANES:(ch + 1) * LANES]
            incl = _lane_cumsum(chunk) + carry
            carry = incl[:, LANES - 1:LANES]
            pos.append(incl - chunk)
        pos = jnp.concatenate(pos, axis=1)
        slot = lax.broadcasted_iota(jnp.int32, (N_SELECT, nblk), 0).astype(F32)
        onehot = jnp.where((pos == slot) & (sel > 0.5), 1.0, 0.0).astype(BF16)
        ids = lax.broadcasted_iota(jnp.int32, (1, nblk), 1).astype(F32).astype(BF16)
        idx = _nt_dot(ids, onehot)
        idx_ref[0, g:g + 1, 0:N_SELECT] = idx.astype(jnp.int32)


def _nsa_sample_sel(hs3, cb):
    n = hs3.shape[0]
    nblk = cb.shape[3]
    assert nblk % LANES == 0 and nblk >= N_PAST_SEL and nblk <= 256
    return pl.pallas_call(
        _nsa_sample_sel_kernel, grid=(n,),
        in_specs=[pl.BlockSpec((1, 1, N_IN_PAD), lambda b: (b, 0, 0)),
                  pl.BlockSpec((1, 2, NSA_KV, nblk, HD), lambda b: (b, 0, 0, 0, 0))],
        out_specs=[pl.BlockSpec((1, NSA_H, HD), lambda b: (b, 0, 0)),
                   pl.BlockSpec((1, NSA_KV, LANES), lambda b: (b, 0, 0))],
        out_shape=[jax.ShapeDtypeStruct((n, NSA_H, HD), F32), jax.ShapeDtypeStruct((n, NSA_KV, LANES), jnp.int32)],
        compiler_params=pltpu.CompilerParams(dimension_semantics=("arbitrary",)),
        name="nsa_sample_select",
    )(hs3, cb)


def _nsa_sample_attn_kernel(idx_ref, pt_ref, hs_ref, oc_ref, win_ref, *refs):
    n_blk = NSA_KV * N_PAST_SEL
    blk_refs, (o_ref, nw_ref) = refs[:n_blk], refs[n_blk:]
    b = pl.program_id(0)
    gates = _sigmoid(_hrow(hs_ref, "ngate"))
    lane_half = lax.broadcasted_iota(jnp.int32, (NSA_R, PAGE_SIZE), 1) // CMP_BLOCK
    outs = []
    for g in range(NSA_KV):
        q4 = _q_group(hs_ref, "nq", g)
        q4b = q4.astype(BF16)
        kn, vn = _hrow(hs_ref, "nslc", g * HD, HD), _hrow(hs_ref, "nslc", KV_W + g * HD, HD)
        s_cur = jnp.sum(q4 * kn, axis=1, keepdims=True)
        s_list, masks = [], []
        for k in range(N_PAST_SEL):
            half = idx_ref[b, g * LANES + k] & 1
            mask = lane_half == half
            s = _nn_dot(q4b, blk_refs[g * N_PAST_SEL + k][0, 0, 0].astype(BF16))
            s_list.append(jnp.where(mask, s, NEG))
            masks.append(mask)
        m = s_cur
        for s in s_list:
            m = jnp.maximum(m, jnp.max(s, axis=1, keepdims=True))
        l = jnp.exp(s_cur - m)
        acc = l * vn
        for k in range(N_PAST_SEL):
            e = jnp.where(masks[k], jnp.exp(s_list[k] - m), 0.0)
            l = l + jnp.sum(e, axis=1, keepdims=True)
            acc = acc + _nt_dot(e.astype(BF16), blk_refs[g * N_PAST_SEL + k][0, 1, 0].astype(BF16))
        o_s = acc / jnp.maximum(l, TINY)
        kwn, vwn = _hrow(hs_ref, "nwin", g * HD, HD), _hrow(hs_ref, "nwin", KV_W + g * HD, HD)
        s_w = _nn_dot(q4b, win_ref[0, 0, g].astype(BF16))
        s_wn = jnp.sum(q4 * kwn, axis=1, keepdims=True)
        m = jnp.maximum(jnp.max(s_w, axis=1, keepdims=True), s_wn)
        e, en = jnp.exp(s_w - m), jnp.exp(s_wn - m)
        l = jnp.sum(e, axis=1, keepdims=True) + en
        o_w = (_nt_dot(e.astype(BF16), win_ref[0, 1, g].astype(BF16)) + en * vwn) / jnp.maximum(l, TINY)
        o_c = oc_ref[0, g * NSA_R:(g + 1) * NSA_R, :]
        for r in range(NSA_R):
            c = g * NSA_R + r
            gate = lambda br: gates[:, br * NSA_H + c: br * NSA_H + c + 1]
            outs.append(gate(0) * o_c[r:r + 1] + gate(1) * o_s[r:r + 1] + gate(2) * o_w[r:r + 1])
    o_ref[0] = jnp.concatenate(outs, axis=1) * _silu(_hrow(hs_ref, "nsilu"))
    wb = win_ref.shape[4]
    newcol = _row_to_col(_hrow(hs_ref, "nwin"))
    last = lax.broadcasted_iota(jnp.int32, (HD, wb), 1) == wb - 1
    for kv in range(2):
        for g in range(NSA_KV):
            r0 = (kv * NSA_KV + g) * HD
            shifted = pltpu.roll(win_ref[0, kv, g], wb - 1, axis=1)
            nw_ref[0, kv, g] = jnp.where(last, newcol[r0:r0 + HD], shifted)


def _nsa_sample_attn(idx, page_table, hs3, o_c, win_state, pool_slc):
    n = hs3.shape[0]
    wb = win_state.shape[4]
    assert wb == WINDOW
    blk_specs = [pl.BlockSpec((1, 2, 1, HD, PAGE_SIZE),
                              lambda b, idx, pt, g=g, k=k: (pt[b, idx[b, g * LANES + k] >> 1], 0, g, 0, 0))
                 for g in range(NSA_KV) for k in range(N_PAST_SEL)]
    grid_spec = pltpu.PrefetchScalarGridSpec(
        num_scalar_prefetch=2, grid=(n,),
        in_specs=[pl.BlockSpec((1, 1, N_IN_PAD), lambda b, idx, pt: (b, 0, 0)),
                  pl.BlockSpec((1, NSA_H, HD), lambda b, idx, pt: (b, 0, 0)),
                  pl.BlockSpec((1, 2, NSA_KV, HD, wb), lambda b, idx, pt: (b, 0, 0, 0, 0))] + blk_specs,
        out_specs=[pl.BlockSpec((1, 1, BR_W), lambda b, idx, pt: (b, 0, 0)),
                   pl.BlockSpec((1, 2, NSA_KV, HD, wb), lambda b, idx, pt: (b, 0, 0, 0, 0))],
    )
    return pl.pallas_call(
        _nsa_sample_attn_kernel, grid_spec=grid_spec,
        out_shape=[jax.ShapeDtypeStruct((n, 1, BR_W), F32), jax.ShapeDtypeStruct((n, 2, NSA_KV, HD, wb), F32)],
        compiler_params=pltpu.CompilerParams(dimension_semantics=("arbitrary",)),
        name="nsa_sample_attn",
    )(idx, page_table, hs3, o_c, win_state, *([pool_slc] * (NSA_KV * N_PAST_SEL)))


FOX_NPG = 16


def _fox_sample_kernel(pt_ref, hs_ref, *refs):
    kv_refs = refs[:FOX_NPG]
    lf_hbm, o_ref, q_b, acc, m_s, l_s, r_s, lf_buf, lf_sem = refs[FOX_NPG:]
    n_pages = r_s.shape[0]
    b, j = pl.program_id(0), pl.program_id(1)

    def lf_copy(b_, p, slot_):
        return pltpu.make_async_copy(lf_hbm.at[pt_ref[b_, p]], lf_buf.at[slot_, p], lf_sem.at[slot_])

    @pl.when(j == 0)
    def _():
        slot = b % 2

        @pl.when(b == 0)
        def _():
            for p in range(n_pages):
                lf_copy(b, p, slot).start()

        @pl.when(b + 1 < pl.num_programs(0))
        def _():
            for p in range(n_pages):
                lf_copy(b + 1, p, 1 - slot).start()

        for h in range(FOX_H):
            q_b[h] = jnp.broadcast_to(_row_to_col(_hrow(hs_ref, "fq", h * HD, HD) * SCALE), (HD, PAGE_SIZE))
        acc[...] = jnp.zeros(acc.shape, F32)
        m_s[...] = jnp.full(m_s.shape, NEG, F32)
        l_s[...] = jnp.zeros(l_s.shape, F32)
        for p in range(n_pages):
            lf_copy(0, p, slot).wait()
        incl = _lane_cumsum(lf_buf[slot].reshape(n_pages * FOX_H, PAGE_SIZE))
        later = _row_to_col(_hrow(hs_ref, "flog"))
        for p in reversed(range(n_pages)):
            inc = incl[p * FOX_H:(p + 1) * FOX_H]
            total = inc[:, PAGE_SIZE - 1:PAGE_SIZE]
            r_s[p] = later + (total - inc)
            later = later + total

    s_list = []
    for i in range(FOX_NPG):
        qk = jnp.concatenate([jnp.sum(kv_refs[i][0, 0, h] * q_b[h], axis=0, keepdims=True) for h in range(FOX_H)], axis=0)
        s_list.append(qk + r_s[j * FOX_NPG + i])
    s_max = s_list[0]
    for s in s_list[1:]:
        s_max = jnp.maximum(s_max, s)
    m_old = m_s[...]
    m_new = jnp.maximum(m_old, jnp.max(s_max, axis=1, keepdims=True))
    alpha = jnp.exp(m_old - m_new)
    p_list = [jnp.exp(s - m_new) for s in s_list]
    p_sum = p_list[0]
    for p in p_list[1:]:
        p_sum = p_sum + p
    m_s[...] = m_new
    l_s[...] = l_s[...] * alpha + jnp.sum(p_sum, axis=1, keepdims=True)
    for h in range(FOX_H):
        a = acc[h] * alpha[h:h + 1, :]
        for i in range(FOX_NPG):
            a = a + kv_refs[i][0, 1, h] * p_list[i][h:h + 1, :]
        acc[h] = a

    @pl.when(j == pl.num_programs(1) - 1)
    def _():
        outs = []
        for h in range(FOX_H):
            q = _hrow(hs_ref, "fq", h * HD, HD) * SCALE
            kn, vn = _hrow(hs_ref, "fkv", h * HD, HD), _hrow(hs_ref, "fkv", FOX_H * HD + h * HD, HD)
            s_n = jnp.sum(q * kn, axis=1, keepdims=True)
            m_h, l_h = m_s[h:h + 1, :], l_s[h:h + 1, :]
            m_f = jnp.maximum(m_h, s_n)
            al, pn = jnp.exp(m_h - m_f), jnp.exp(s_n - m_f)
            past = _col_to_row(jnp.sum(acc[h], axis=1, keepdims=True))
            outs.append((past * al + pn * vn) / jnp.maximum(l_h * al + pn, TINY))
        o_ref[0] = jnp.concatenate(outs, axis=1) * _silu(_hrow(hs_ref, "fsilu"))


def _fox_sample(page_table, hs3, pool_kv, pool_lf):
    n, n_pages = page_table.shape
    assert n_pages % FOX_NPG == 0
    ng = n_pages // FOX_NPG
    kv_specs = [pl.BlockSpec((1, 2, FOX_H, HD, PAGE_SIZE), lambda b, j, pt, i=i: (pt[b, j * FOX_NPG + i], 0, 0, 0, 0))
                for i in range(FOX_NPG)]
    grid_spec = pltpu.PrefetchScalarGridSpec(
        num_scalar_prefetch=1, grid=(n, ng),
        in_specs=[pl.BlockSpec((1, 1, N_IN_PAD), lambda b, j, pt: (b, 0, 0))] + kv_specs
                 + [pl.BlockSpec(memory_space=pl.ANY)],
        out_specs=pl.BlockSpec((1, 1, BR_W), lambda b, j, pt: (b, 0, 0)),
        scratch_shapes=[pltpu.VMEM((FOX_H, HD, PAGE_SIZE), F32), pltpu.VMEM((FOX_H, HD, PAGE_SIZE), F32),
                        pltpu.VMEM((FOX_H, 1), F32), pltpu.VMEM((FOX_H, 1), F32),
                        pltpu.VMEM((n_pages, FOX_H, PAGE_SIZE), F32),
                        pltpu.VMEM((2, n_pages, FOX_H, PAGE_SIZE), F32), pltpu.SemaphoreType.DMA((2,))],
    )
    return pl.pallas_call(
        _fox_sample_kernel, grid_spec=grid_spec,
        out_shape=jax.ShapeDtypeStruct((n, 1, BR_W), F32),
        compiler_params=pltpu.CompilerParams(dimension_semantics=("arbitrary", "arbitrary"),
                                             vmem_limit_bytes=VMEM_LIMIT),
        name="fox_sample",
    )(page_table, hs3, *([pool_kv] * FOX_NPG), pool_lf)


def _split_weights(w_in):
    wt = w_in.T
    sl = lambda k: wt[_COLS[k][0]:_COLS[k][1]]
    d = wt.shape[1]
    wq = jnp.concatenate([sl(k) for k in _QGROUPS] + [sl("bgate"), sl("ngate"),
                                                        jnp.zeros((LANES - 24, d), wt.dtype)], axis=0).astype(BF16)
    wkv = jnp.concatenate([sl(k) for k, _ in _KVGROUPS], axis=0).astype(BF16)
    w_all = jnp.concatenate([sl(k) for k in _S_ORDER] + [jnp.zeros((N_IN_PAD - N_IN, d), wt.dtype)],
                            axis=0).astype(BF16)
    return wq, wkv, w_all


def kernel(x_prompt, x_sample, cache_nsa_cmp, cache_nsa_slc, state_nsa_win, cache_fox_kv, cache_fox_logf, cache_mem_kv, page_table, mem_prompt, g_pre, g_post, g_mem, w_in, b_fgate, w_mem_kv, w_cmp1, w_cmp2, pe_cmp, w_branch, w_out):
    assert w_in.shape[0] == 1 and x_sample.shape[1] == 1, "one layer, one new token per sequence"
    b, t, d = x_prompt.shape
    n_s, n_pages = page_table.shape
    past = n_pages * PAGE_SIZE
    wq, wkv, w_all = _split_weights(w_in[0])
    w1bd, w2bd, pe_t = _compress_weights(w_cmp1[0], w_cmp2[0], pe_cmp[0])
    wb, wo = w_branch[0].astype(BF16), w_out[0].astype(BF16)

    (nq, nsilu, fq, fsilu, mq, msilu, bgate, ngate, cmp_t, slc_t, win_t, fkv_t, flog_t) = _proj_prompt(
        x_prompt, g_pre, wq, wkv, _rope_tables(jnp.arange(t, dtype=F32)), b_fgate[0])
    cb_p = _compress(cmp_t.reshape(b, 2, NSA_KV, HD, t), jnp.zeros((1, 1), jnp.int32), w1bd, w2bd, pe_t,
                     n_batch=b, n_pages=t // PAGE_SIZE, paged=False)
    cb_p = cb_p.reshape(b, 2, NSA_KV, t // CMP_BLOCK, HD)
    a_nsa = _nsa_prompt(nq, slc_t, win_t, cb_p, cb_p, ngate, nsilu)
    a_fox = _fox_prompt(fq, fkv_t, flog_t, fsilu)
    mkv_p = _mem_kv(mem_prompt, g_mem, w_mem_kv[0].astype(BF16))
    a_mem = _mem_attn(mq, msilu, mkv_p)
    y_p = _merge(a_nsa, a_fox, a_mem, bgate, x_prompt, wb, wo, g_post)

    rows_minor = lambda a: jnp.transpose(a, (0, 2, 3, 4, 1))
    hs = _proj_sample(x_sample[:, 0, :], g_pre, w_all, _rope_tables(jnp.full((n_s,), past, F32)), b_fgate[0])
    hs3 = hs.reshape(n_s, 1, N_IN_PAD)
    cb_s = _compress(rows_minor(cache_nsa_cmp[0]), page_table, w1bd, w2bd, pe_t,
                     n_batch=n_s, n_pages=n_pages, paged=True)
    cb_s = cb_s.reshape(n_s, 2, NSA_KV, past // CMP_BLOCK, HD)
    o_c, idx = _nsa_sample_sel(hs3, cb_s)
    a_nsa_s, new_win = _nsa_sample_attn(idx.reshape(n_s, NSA_KV * LANES), page_table, hs3, o_c,
                                        rows_minor(state_nsa_win[0]), rows_minor(cache_nsa_slc[0]))
    a_fox_s = _fox_sample(page_table, hs3, rows_minor(cache_fox_kv[0]),
                          jnp.transpose(cache_fox_logf[0], (0, 2, 1)))
    col = lambda k: hs[:, _S_OFF[k]:_S_OFF[k] + _COLS[k][1] - _COLS[k][0]]
    a_mem_s = _mem_attn(col("mq").reshape(n_s, 1, BR_W), col("msilu").reshape(n_s, 1, BR_W),
                        cache_mem_kv[0].reshape(n_s, -1, MEM_HD))
    y_s = _merge(a_nsa_s.reshape(1, n_s, BR_W), a_fox_s.reshape(1, n_s, BR_W), a_mem_s.reshape(1, n_s, BR_W),
                 col("bgate").reshape(1, n_s, 3 * d), x_sample.reshape(1, n_s, d), wb, wo, g_post)

    def rows_major(a_t, heads):
        n, _, r = a_t.shape
        return jnp.transpose(a_t.reshape(n, 2, heads, HD, r), (0, 4, 1, 2, 3))[None]

    wkeep = min(WINDOW, t)
    return (y_p, y_s.reshape(n_s, 1, d),
            rows_major(cmp_t, NSA_KV), rows_major(slc_t, NSA_KV), rows_major(win_t[:, :, t - wkeep:], NSA_KV),
            rows_major(fkv_t, FOX_H), jnp.transpose(flog_t, (0, 2, 1))[None],
            mkv_p.reshape(1, b, -1, 2, MEM_H, MEM_HD),
            col("ncmp").reshape(1, n_s, 1, 2, NSA_KV, HD), col("nslc").reshape(1, n_s, 1, 2, NSA_KV, HD),
            jnp.transpose(new_win, (0, 4, 1, 2, 3))[None],
            col("fkv").reshape(1, n_s, 1, 2, FOX_H, HD), col("flog").reshape(1, n_s, 1, FOX_H))
```

```python
import functools

import numpy as np
import jax
import jax.numpy as jnp
from jax import lax
from jax.experimental import pallas as pl
from jax.experimental.pallas import tpu as pltpu

HD = 64
NSA_H = 8
NSA_KV = 2
NSA_R = NSA_H // NSA_KV
CMP_BLOCK = 64
N_SELECT = 16
WINDOW = 512
FOX_H = 8
MEM_H = 4
MEM_HD = 128
BR_W = 512
ROT_DIM = HD // 4
ROT_HALF = ROT_DIM // 2
ROPE_THETA = 500000.0
PAGE_SIZE = 128
EPS = 1e-6
NEG = -1e30
BIG = 1e4
TINY = 1e-30
KV_W = NSA_KV * HD
SCALE = HD ** -0.5

LANES = 128
SUBLANES = 8
VMEM_LIMIT = 56 * 1024 * 1024

_COLS = dict(nq=(0, 512), ncmp=(512, 768), nslc=(768, 1024), nwin=(1024, 1280), ngate=(1280, 1304),
             nsilu=(1304, 1816), fq=(1816, 2328), fkv=(2328, 3352), flog=(3352, 3360), fsilu=(3360, 3872),
             mq=(3872, 4384), msilu=(4384, 4896), bgate=(4896, 7968))
N_IN = 7968
N_IN_PAD = 8064

F32 = jnp.float32
BF16 = jnp.bfloat16


def _nt_dot(a, b):
    return lax.dot_general(a, b, (((1,), (1,)), ((), ())), preferred_element_type=F32)


def _nn_dot(a, b):
    return lax.dot_general(a, b, (((1,), (0,)), ((), ())), preferred_element_type=F32)


def _rmsnorm_rows(x, g):
    var = jnp.mean(x * x, axis=-1, keepdims=True)
    return x * lax.rsqrt(var + EPS) * g


def _silu(x):
    return x * (1.0 / (1.0 + jnp.exp(-x)))


def _sigmoid(x):
    return 1.0 / (1.0 + jnp.exp(-x))


def _rope_tables(pos):
    inv_freq = ROPE_THETA ** (-jnp.arange(ROT_HALF, dtype=F32) / ROT_HALF)
    ang = pos[:, None] * inv_freq[None, :]
    cos, sin = jnp.cos(ang), jnp.sin(ang)
    t = pos.shape[0]
    one = jnp.ones((t, HD - ROT_DIM), F32)
    zero = jnp.zeros((t, HD - ROT_HALF), F32)
    c_head = jnp.concatenate([cos, cos, one], axis=1)
    s_hi = jnp.concatenate([-sin, zero], axis=1)
    s_lo = jnp.concatenate([jnp.zeros((t, ROT_HALF), F32), sin, jnp.zeros((t, HD - ROT_DIM), F32)], axis=1)
    tile2 = lambda a: jnp.concatenate([a, a], axis=1)
    return tile2(c_head), tile2(s_hi), tile2(s_lo), cos, sin


def _rope_lanes(x, c, s_hi, s_lo):
    k = x.shape[1] // LANES
    if k > 1:
        c = jnp.concatenate([c] * k, axis=1)
        s_hi = jnp.concatenate([s_hi] * k, axis=1)
        s_lo = jnp.concatenate([s_lo] * k, axis=1)
    n = x.shape[1]
    up = pltpu.roll(x, n - ROT_HALF, axis=1)
    dn = pltpu.roll(x, ROT_HALF, axis=1)
    return x * c + up * s_hi + dn * s_lo


_QGROUPS = ("nq", "nsilu", "fq", "fsilu", "mq", "msilu")
_KVGROUPS = (("ncmp", 256), ("nslc", 256), ("nwin", 256), ("fkv", 1024), ("flog", 8))
NQ_ROWS = 6 * 512 + 3072 + LANES
NKV_ROWS = 256 * 3 + 1024 + 8
PROJ_TM = 512


def _proj_prompt_kernel(x_ref, g_ref, wq_ref, wkv_ref, c_ref, shi_ref, slo_ref, cosT_ref, sinT_ref, bf_ref,
                        nq_ref, nsilu_ref, fq_ref, fsilu_ref, mq_ref, msilu_ref, bgate_ref, ngate_ref,
                        cmpT_ref, slcT_ref, winT_ref, fkvT_ref, flogT_ref):
    xb = _rmsnorm_rows(x_ref[0], g_ref[...]).astype(BF16)
    outs = (nq_ref, nsilu_ref, fq_ref, fsilu_ref, mq_ref, msilu_ref)
    for i, o_ref in enumerate(outs):
        h = _nt_dot(xb, wq_ref[i * 512:(i + 1) * 512, :])
        if i == 0:
            h = _rope_lanes(h, c_ref[...], shi_ref[...], slo_ref[...])
        if i in (0, 2):
            h = h * (SCALE * LOG2E)
        o_ref[0] = h.astype(o_ref.dtype)
    off = 6 * 512
    for j in range(3072 // 512):
        h = _nt_dot(xb, wq_ref[off + j * 512: off + (j + 1) * 512, :])
        bgate_ref[0, :, j * 512:(j + 1) * 512] = h.astype(bgate_ref.dtype)
    off += 3072
    ngate_ref[0] = _nt_dot(xb, wq_ref[off:off + LANES, :])
    cosT, sinT = cosT_ref[...], sinT_ref[...]
    roff = 0
    for o_ref in (cmpT_ref, slcT_ref, winT_ref):
        hT = _nt_dot(wkv_ref[roff:roff + 256, :], xb)
        o_ref[0] = hT
        for g in range(NSA_KV):
            x1 = hT[g * HD: g * HD + ROT_HALF, :]
            x2 = hT[g * HD + ROT_HALF: g * HD + ROT_DIM, :]
            o_ref[0, g * HD: g * HD + ROT_HALF, :] = x1 * cosT - x2 * sinT
            o_ref[0, g * HD + ROT_HALF: g * HD + ROT_DIM, :] = x2 * cosT + x1 * sinT
        roff += 256
    for j in range(1024 // 256):
        fkvT_ref[0, j * 256:(j + 1) * 256, :] = _nt_dot(wkv_ref[roff + j * 256: roff + (j + 1) * 256, :], xb)
    roff += 1024
    flogT_ref[0] = _log_sigmoid(_nt_dot(wkv_ref[roff:roff + 8, :], xb) + bf_ref[...])


def _proj_prompt(x, g_pre, wq, wkv, tabs, b_fgate):
    b, t, d = x.shape
    tm = min(PROJ_TM, t)
    nt = t // tm
    c, shi, slo, cos, sin = tabs
    cosT, sinT = cos.T, sin.T
    row = lambda bi, ti: (bi, ti, 0)
    col = lambda bi, ti: (bi, 0, ti)
    const = lambda bi, ti: (0, 0)
    tab = lambda bi, ti: (ti, 0)
    tabT = lambda bi, ti: (0, ti)
    resident = dict(pipeline_mode=pl.Buffered(1))
    in_specs = [
        pl.BlockSpec((1, tm, d), row),
        pl.BlockSpec((1, d), const),
        pl.BlockSpec((NQ_ROWS, d), const, **resident),
        pl.BlockSpec((NKV_ROWS, d), const, **resident),
        pl.BlockSpec((tm, LANES), tab), pl.BlockSpec((tm, LANES), tab), pl.BlockSpec((tm, LANES), tab),
        pl.BlockSpec((ROT_HALF, tm), tabT), pl.BlockSpec((ROT_HALF, tm), tabT),
        pl.BlockSpec((FOX_H, 1), const),
    ]
    q_shape = jax.ShapeDtypeStruct((b, t, 512), BF16)
    out_shape = [q_shape] * 6 + [
        jax.ShapeDtypeStruct((b, t, 3072), BF16),
        jax.ShapeDtypeStruct((b, t, LANES), F32),
        jax.ShapeDtypeStruct((b, 256, t), F32), jax.ShapeDtypeStruct((b, 256, t), F32),
        jax.ShapeDtypeStruct((b, 256, t), F32),
        jax.ShapeDtypeStruct((b, 1024, t), F32),
        jax.ShapeDtypeStruct((b, 8, t), F32),
    ]
    out_specs = [pl.BlockSpec((1, tm, 512), row)] * 6 + [
        pl.BlockSpec((1, tm, 3072), row),
        pl.BlockSpec((1, tm, LANES), row),
        pl.BlockSpec((1, 256, tm), col), pl.BlockSpec((1, 256, tm), col), pl.BlockSpec((1, 256, tm), col),
        pl.BlockSpec((1, 1024, tm), col),
        pl.BlockSpec((1, 8, tm), col),
    ]
    return pl.pallas_call(
        _proj_prompt_kernel,
        grid=(b, nt),
        in_specs=in_specs,
        out_specs=out_specs,
        out_shape=out_shape,
        compiler_params=pltpu.CompilerParams(dimension_semantics=("arbitrary", "arbitrary"),
                                             vmem_limit_bytes=VMEM_LIMIT),
        name="proj_prompt",
    )(x, g_pre, wq, wkv, c, shi, slo, cosT, sinT, b_fgate.reshape(FOX_H, 1))


PROJ_S_TN = 1152


def _proj_sample_kernel(x_ref, g_ref, w_ref, c_ref, shi_ref, slo_ref, bf_ref, o_ref):
    xb = _rmsnorm_rows(x_ref[...], g_ref[...]).astype(BF16)
    o_ref[...] = _nt_dot(xb, w_ref[...])

    @pl.when(pl.program_id(0) == _S_OFF["flog"] // PROJ_S_TN)
    def _():
        lo = _S_OFF["flog"] % PROJ_S_TN
        o_ref[:, lo:lo + FOX_H] = _log_sigmoid(o_ref[:, lo:lo + FOX_H] + bf_ref[...])

    @pl.when(pl.program_id(0) == 0)
    def _():
        c, shi, slo = c_ref[...], shi_ref[...], slo_ref[...]
        for c0 in (0, 128, 256, 384, 512, 768, 1024):
            o_ref[:, c0:c0 + LANES] = _rope_lanes(o_ref[:, c0:c0 + LANES], c, shi, slo)


def _proj_sample(x, g_pre, w_all, tabs, b_fgate):
    n, d = x.shape
    c, shi, slo, _, _ = tabs
    const = lambda j: (0, 0)
    return pl.pallas_call(
        _proj_sample_kernel,
        grid=(N_IN_PAD // PROJ_S_TN,),
        in_specs=[pl.BlockSpec((n, d), const), pl.BlockSpec((1, d), const),
                  pl.BlockSpec((PROJ_S_TN, d), lambda j: (j, 0)),
                  pl.BlockSpec((n, LANES), const), pl.BlockSpec((n, LANES), const), pl.BlockSpec((n, LANES), const),
                  pl.BlockSpec((1, FOX_H), const)],
        out_specs=pl.BlockSpec((n, PROJ_S_TN), lambda j: (0, j)),
        out_shape=jax.ShapeDtypeStruct((n, N_IN_PAD), F32),
        compiler_params=pltpu.CompilerParams(dimension_semantics=("arbitrary",)),
        name="proj_sample",
    )(x, g_pre, w_all, c, shi, slo, b_fgate.reshape(1, FOX_H))


CMP_PITCH = 72
CMP_DCHUNK = 8


def _compress_kernel(pt_ref, src_ref, w1_ref, w2_ref, pe_ref, o_ref, buf, sem, *, n_pages, paged):
    bi, kv = pl.program_id(0), pl.program_id(1)
    n_steps = pl.num_programs(0) * 2
    step = bi * 2 + kv
    slot = step % 2

    def page_copy(b_, kv_, i, slot_):
        if paged:
            src = src_ref.at[pt_ref[b_, i], kv_]
        else:
            src = src_ref.at[b_, kv_, :, :, pl.ds(i * PAGE_SIZE, PAGE_SIZE)]
        dst = buf.at[slot_, :, pl.ds(i * CMP_PITCH, HD), :]
        return pltpu.make_async_copy(src, dst, sem.at[slot_])

    def issue(b_, kv_, slot_):
        for i in range(n_pages):
            page_copy(b_, kv_, i, slot_).start()

    @pl.when(step == 0)
    def _():
        issue(bi, kv, slot)

    @pl.when(step + 1 < n_steps)
    def _():
        nxt = step + 1
        issue(nxt // 2, nxt % 2, 1 - slot)

    for i in range(n_pages):
        page_copy(0, 0, i, slot).wait()

    acc = jnp.zeros((2 * n_pages, LANES), F32)
    for c in range(HD // CMP_DCHUNK):
        cols = []
        for dd in range(CMP_DCHUNK):
            d = c * CMP_DCHUNK + dd
            rows = [buf[slot, g, pl.ds(d, n_pages, stride=CMP_PITCH), :] for g in range(NSA_KV)]
            x = jnp.concatenate(rows, axis=0) + pe_ref[0, d:d + 1, :]
            cols.append(x.astype(BF16))
        lhs = jnp.concatenate(cols, axis=1)
        k0 = c * CMP_DCHUNK * LANES
        acc = acc + _nn_dot(lhs, w1_ref[0, k0:k0 + CMP_DCHUNK * LANES, :])
    hid = _silu(acc)
    out = _nn_dot(hid.astype(BF16), w2_ref[0])
    o_ref[0, 0, 0] = out[:n_pages]
    o_ref[0, 0, 1] = out[n_pages:]


def _compress(src, page_table, w1bd, w2bd, pe_t, *, n_batch, n_pages, paged):
    kern = functools.partial(_compress_kernel, n_pages=n_pages, paged=paged)
    grid_spec = pltpu.PrefetchScalarGridSpec(
        num_scalar_prefetch=1,
        grid=(n_batch, 2),
        in_specs=[pl.BlockSpec(memory_space=pl.ANY),
                  pl.BlockSpec((1, HD * LANES, LANES), lambda b, kv, pt: (kv, 0, 0)),
                  pl.BlockSpec((1, LANES, LANES), lambda b, kv, pt: (kv, 0, 0)),
                  pl.BlockSpec((1, HD, LANES), lambda b, kv, pt: (kv, 0, 0))],
        out_specs=pl.BlockSpec((1, 1, NSA_KV, n_pages, LANES), lambda b, kv, pt: (b, kv, 0, 0, 0)),
        scratch_shapes=[pltpu.VMEM((2, NSA_KV, n_pages * CMP_PITCH, LANES), F32),
                        pltpu.SemaphoreType.DMA((2,))],
    )
    return pl.pallas_call(
        kern, grid_spec=grid_spec,
        out_shape=jax.ShapeDtypeStruct((n_batch, 2, NSA_KV, n_pages, LANES), F32),
        compiler_params=pltpu.CompilerParams(dimension_semantics=("arbitrary", "arbitrary"),
                                             vmem_limit_bytes=VMEM_LIMIT),
        name="compress_paged" if paged else "compress_prompt",
    )(page_table, src, w1bd, w2bd, pe_t)


def _compress_weights(w1, w2, pe):
    eye2 = jnp.eye(2, dtype=w1.dtype)
    base = jnp.transpose(w1, (0, 2, 1, 3)).astype(BF16)
    zero = jnp.zeros_like(base)
    w1bd = jnp.stack([jnp.concatenate([base, zero], axis=-1), jnp.concatenate([zero, base], axis=-1)], axis=2)
    w1bd = w1bd.reshape(2, HD * LANES, LANES)
    w2bd = jnp.einsum('kef,nm->knemf', w2, eye2).reshape(2, LANES, LANES).astype(BF16)
    pe_t = jnp.tile(jnp.transpose(pe, (0, 2, 1)), (1, 1, 2))
    return w1bd, w2bd, pe_t


def _log_sigmoid(x):
    return jnp.minimum(x, 0.0) - jnp.log1p(jnp.exp(-jnp.abs(x)))


def _lane_cumsum(x):
    lane = lax.broadcasted_iota(jnp.int32, x.shape, 1)
    s = 1
    while s < LANES:
        x = x + jnp.where(lane >= s, pltpu.roll(x, s, axis=1), 0.0)
        s *= 2
    return x


M_INIT = -1e29


def _online_update(carry, s, v_t):
    m, acc = carry
    m_new = jnp.maximum(m, jnp.max(s, axis=0, keepdims=True))
    alpha = jnp.exp2(m - m_new)
    pb = jnp.exp2(s - m_new).astype(BF16)
    if isinstance(v_t, (list, tuple)):
        w = pb.shape[1] // len(v_t)
        pv = jnp.concatenate([_nn_dot(v, pb[:, j * w:(j + 1) * w]) for j, v in enumerate(v_t)], axis=1)
    else:
        pv = _nn_dot(v_t, pb)
    return m_new, acc * alpha + pv


V_AUG = HD + 16
LOG2E = 1.4426950408889634


def _online_init(nq):
    return (jnp.full((1, nq), M_INIT, F32), jnp.zeros((V_AUG, nq), F32))


def _augment_values(v):
    n = v.shape[1]
    tail = jnp.where(lax.broadcasted_iota(jnp.int32, (V_AUG - HD, n), 0) == 0, 1.0, 0.0)
    return jnp.concatenate([v, tail], axis=0).astype(BF16)


def _mask_bias(cond):
    return jnp.where(cond, 0.0, NEG)


def _online_finish(carry):
    _, acc = carry
    return acc[:HD] / jnp.maximum(acc[HD:HD + 1], TINY)


ATT_T = 256
FOX_HPS = 4


def _fox_prompt_kernel(q_ref, k_ref, v_ref, logf_ref, gate_ref, o_ref, k_r, v_t, v_t2, c_b):
    hg, qi = pl.program_id(1), pl.program_id(2)
    tk = ATT_T
    tq = ATT_T
    t = k_ref.shape[2]
    nkt = t // tk
    nh = FOX_HPS

    @pl.when(qi == 0)
    def _():
        for k2 in range(nkt // 2):
            sl = slice(2 * k2 * tk, 2 * (k2 + 1) * tk)
            k_r[k2] = k_ref[0, :, sl].T.astype(BF16)
            v = jnp.concatenate([_augment_values(v_ref[0, j * HD:(j + 1) * HD, sl]) for j in range(nh)], axis=0)
            v_t2[k2] = v
            v_t[2 * k2] = v[:, :tk]
            v_t[2 * k2 + 1] = v[:, tk:]
        carry = jnp.zeros((FOX_H, 1), F32)
        per = 2 * tk // LANES
        for ch in range(t // LANES):
            x = _lane_cumsum(logf_ref[0, :, ch * LANES:(ch + 1) * LANES]) + carry
            carry = x[:, LANES - 1:LANES]
            head = lax.broadcasted_iota(jnp.int32, x.shape, 0)
            for j in range(nh):
                xr = jnp.sum(jnp.where(head == nh * hg + j, x, 0.0), axis=0, keepdims=True) * LOG2E
                c_b[j, ch // per, (ch % per) * LANES:(ch % per + 1) * LANES, :] = jnp.broadcast_to(xr, (LANES, LANES)).T

    q_t = q_ref[0].astype(F32).T.astype(BF16)
    row_head = lax.broadcasted_iota(jnp.int32, q_t.shape, 0) // HD
    q_bd = jnp.concatenate([jnp.where(row_head == j, q_t, jnp.zeros_like(q_t)) for j in range(nh)], axis=1)
    def rows(kt):
        return pl.ds(pl.multiple_of((kt & 1) * tk, tk), tk)

    c_0 = [c_b[j, qi >> 1, rows(qi), :][0:1, :] for j in range(nh)]

    def logits(keys, c_keys):
        bias = jnp.concatenate([c_keys(j) - c_0[j] for j in range(nh) for _ in range(tq // LANES)], axis=1)
        return _nn_dot(keys, q_bd) - bias

    def single(kt):
        return logits(k_r[kt >> 1, rows(kt), :], lambda j: c_b[j, kt >> 1, rows(kt), :])

    def values(v):
        return [v[j * V_AUG:(j + 1) * V_AUG, :] for j in range(nh)]

    def body(k2, carry):
        return _online_update(carry, logits(k_r[k2], lambda j: c_b[j, k2]), values(v_t2[k2]))

    carry = lax.fori_loop(0, qi >> 1, body, _online_init(nh * tq))
    carry = lax.cond((qi & 1) == 1, lambda c: _online_update(c, single(qi - 1), values(v_t[qi - 1])),
                     lambda c: c, carry)
    causal = _mask_bias(lax.broadcasted_iota(jnp.int32, (tk, tq), 0) <= lax.broadcasted_iota(jnp.int32, (tk, tq), 1))
    o_t = _online_finish(_online_update(carry, single(qi) + jnp.concatenate([causal] * nh, axis=1), values(v_t[qi])))
    o = jnp.concatenate([o_t[:, j * tq:(j + 1) * tq] for j in range(nh)], axis=0).T
    o_ref[0] = (o * _silu(gate_ref[0].astype(F32))).astype(o_ref.dtype)


def _fox_prompt(fq, fkv_t, logf_t, fsilu):
    b, t, _ = fq.shape
    tq = ATT_T
    nkt = t // tq
    ng = FOX_H // FOX_HPS
    gw = FOX_HPS * HD
    return pl.pallas_call(
        _fox_prompt_kernel,
        grid=(b, ng, t // tq),
        in_specs=[pl.BlockSpec((1, tq, gw), lambda bi, hg, qi: (bi, qi, hg)),
                  pl.BlockSpec((1, gw, t), lambda bi, hg, qi: (bi, hg, 0)),
                  pl.BlockSpec((1, gw, t), lambda bi, hg, qi: (bi, ng + hg, 0)),
                  pl.BlockSpec((1, FOX_H, t), lambda bi, hg, qi: (bi, 0, 0)),
                  pl.BlockSpec((1, tq, gw), lambda bi, hg, qi: (bi, qi, hg))],
        out_specs=pl.BlockSpec((1, tq, gw), lambda bi, hg, qi: (bi, qi, hg)),
        out_shape=jax.ShapeDtypeStruct((b, t, BR_W), BF16),
        scratch_shapes=[pltpu.VMEM((nkt // 2, 2 * tq, gw), BF16), pltpu.VMEM((nkt, FOX_HPS * V_AUG, tq), BF16),
                        pltpu.VMEM((nkt // 2, FOX_HPS * V_AUG, 2 * tq), BF16),
                        pltpu.VMEM((FOX_HPS, nkt // 2, 2 * tq, LANES), F32)],
        compiler_params=pltpu.CompilerParams(dimension_semantics=("arbitrary",) * 3, vmem_limit_bytes=VMEM_LIMIT),
        name="fox_prompt",
    )(fq, fkv_t, fkv_t, logf_t, fsilu)


def _nsa_prompt_kernel(q_ref, ks_ref, vs_ref, kw_ref, vw_ref, kc_ref, vc_ref, ngate_ref, gate_ref, o_ref,
                       ks_r, vs_t, vs_t2, kw_r, vw_t):
    g, qi = pl.program_id(1), pl.program_id(2)
    tk = ATT_T
    tq = ATT_T
    t = ks_ref.shape[2]
    nkt = t // tk
    nb = t // CMP_BLOCK
    nq = NSA_R * tq

    @pl.when(qi == 0)
    def _():
        for k2 in range(nkt // 2):
            sl = slice(2 * k2 * tk, 2 * (k2 + 1) * tk)
            k_rows = jnp.concatenate([ks_ref[0, :, sl], kw_ref[0, :, sl]], axis=0).T
            ks_r[k2] = k_rows[:, :HD].astype(BF16)
            vs = _augment_values(vs_ref[0, :, sl])
            vw = _augment_values(vw_ref[0, :, sl])
            vs_t2[k2] = vs
            for half in range(2):
                kt = 2 * k2 + half
                kw_r[kt] = k_rows[half * tk:(half + 1) * tk, HD:].astype(BF16)
                vs_t[kt] = vs[:, half * tk:(half + 1) * tk]
                vw_t[kt] = vw[:, half * tk:(half + 1) * tk]

    q_t = q_ref[0].astype(F32).T
    q4 = jnp.concatenate([q_t[r * HD:(r + 1) * HD] for r in range(NSA_R)], axis=1).astype(BF16)
    q0 = qi * tq
    tpos4 = q0 + lax.broadcasted_iota(jnp.int32, (nb, nq), 1) % tq
    blk4 = lax.broadcasted_iota(jnp.int32, (nb, nq), 0)
    readable = (blk4 + 1) * CMP_BLOCK - 1 <= tpos4
    s_c = jnp.where(readable, _nn_dot(kc_ref[0, 0, 0].astype(BF16), q4), NEG)
    m_c = jnp.max(s_c, axis=0, keepdims=True)
    e_c = jnp.where(readable, jnp.exp2(s_c - m_c), 0.0)
    p_c = e_c / jnp.maximum(jnp.sum(e_c, axis=0, keepdims=True), TINY)
    eye = jnp.where(lax.broadcasted_iota(jnp.int32, (HD, HD), 0) == lax.broadcasted_iota(jnp.int32, (HD, HD), 1),
                    1.0, 0.0).astype(BF16)
    vc_t = _nt_dot(eye, vc_ref[0, 0, 0].astype(BF16)).astype(BF16)
    o_c = _nn_dot(vc_t, p_c.astype(BF16))
    imp = p_c[:, 0:tq] + p_c[:, tq:2 * tq] + p_c[:, 2 * tq:3 * tq] + p_c[:, 3 * tq:4 * tq]
    tpos = q0 + lax.broadcasted_iota(jnp.int32, (nb, tq), 1)
    blk = lax.broadcasted_iota(jnp.int32, (nb, tq), 0)
    cur = tpos // CMP_BLOCK
    score = jnp.where(blk < cur, imp, jnp.where(blk == cur, BIG, NEG))
    rank = jnp.zeros((nb, tq), F32)
    for mth in range(nb):
        rowv = score[mth:mth + 1, :]
        ahead = (rowv > score) | ((rowv == score) & (blk > mth))
        rank = rank + jnp.where(ahead, 1.0, 0.0)
    sel_bias = _mask_bias((rank < N_SELECT) & (score > 0.5 * NEG)).astype(BF16)

    key = lax.broadcasted_iota(jnp.int32, (tk, tq), 0)
    qry = lax.broadcasted_iota(jnp.int32, (tk, tq), 1)
    heads = lambda a: jnp.concatenate([a] * NSA_R, axis=1)

    def sel_block_bias(k0, n):
        kb = (k0 + lax.broadcasted_iota(jnp.int32, (n, nb), 0)) // CMP_BLOCK
        expand = jnp.where(kb == lax.broadcasted_iota(jnp.int32, (n, nb), 1), 1.0, 0.0).astype(BF16)
        return _nn_dot(expand, sel_bias)

    def sel_keys(kt):
        return ks_r[kt >> 1, pl.ds(pl.multiple_of((kt & 1) * tk, tk), tk), :]

    def sel_body(k2, carry):
        s = _nn_dot(ks_r[k2], q4) + heads(sel_block_bias(k2 * 2 * tk, 2 * tk))
        return _online_update(carry, s, vs_t2[k2])

    def sel_single(carry):
        kt = qi - 1
        s = _nn_dot(sel_keys(kt), q4) + heads(sel_block_bias(kt * tk, tk))
        return _online_update(carry, s, vs_t[kt])

    carry = lax.fori_loop(0, qi >> 1, sel_body, _online_init(nq))
    carry = lax.cond((qi & 1) == 1, sel_single, lambda c: c, carry)
    s, bias = _nn_dot(sel_keys(qi), q4), sel_block_bias(qi * tk, tk)
    carry = _online_update(carry, s + heads(bias + _mask_bias(key <= qry)), vs_t[qi])
    o_s = _online_finish(carry)
    carry = _online_init(nq)
    k2 = jnp.maximum(qi - 2, 0)
    carry = _online_update(carry, _nn_dot(kw_r[k2], q4) + heads(_mask_bias((key >= qry) & (qi >= 2))), vw_t[k2])
    k1 = jnp.maximum(qi - 1, 0)
    carry = _online_update(carry, _nn_dot(kw_r[k1], q4) + _mask_bias(qi >= 1), vw_t[k1])
    carry = _online_update(carry, _nn_dot(kw_r[qi], q4) + heads(_mask_bias(key <= qry)), vw_t[qi])
    o_w = _online_finish(carry)
    gts = _sigmoid(ngate_ref[0]).T

    def gate_row(br, r):
        c0 = br * NSA_H + r
        return jnp.where(g == 0, gts[c0:c0 + 1, :], gts[c0 + NSA_R:c0 + NSA_R + 1, :])

    outs = []
    for r in range(NSA_R):
        sl = slice(r * tq, (r + 1) * tq)
        outs.append(gate_row(0, r) * o_c[:, sl] + gate_row(1, r) * o_s[:, sl] + gate_row(2, r) * o_w[:, sl])
    o = jnp.concatenate(outs, axis=0).T * _silu(gate_ref[0].astype(F32))
    o_ref[0] = o.astype(o_ref.dtype)


def _nsa_prompt(nq, slc_t, win_t, kcb, vcb, ngate, nsilu):
    b, t, _ = nq.shape
    tq = ATT_T
    nkt = t // tq
    nb = t // CMP_BLOCK
    gw = NSA_R * HD
    kv_spec = lambda kv: pl.BlockSpec((1, HD, t), lambda bi, g, qi, kv=kv: (bi, kv * NSA_KV + g, 0))
    cb_spec = lambda kv: pl.BlockSpec((1, 1, 1, nb, HD), lambda bi, g, qi, kv=kv: (bi, kv, g, 0, 0))
    return pl.pallas_call(
        _nsa_prompt_kernel,
        grid=(b, NSA_KV, t // tq),
        in_specs=[pl.BlockSpec((1, tq, gw), lambda bi, g, qi: (bi, qi, g)),
                  kv_spec(0), kv_spec(1), kv_spec(0), kv_spec(1),
                  cb_spec(0), cb_spec(1),
                  pl.BlockSpec((1, tq, LANES), lambda bi, g, qi: (bi, qi, 0)),
                  pl.BlockSpec((1, tq, gw), lambda bi, g, qi: (bi, qi, g))],
        out_specs=pl.BlockSpec((1, tq, gw), lambda bi, g, qi: (bi, qi, g)),
        out_shape=jax.ShapeDtypeStruct((b, t, BR_W), BF16),
        scratch_shapes=[pltpu.VMEM((nkt // 2, 2 * tq, HD), BF16), pltpu.VMEM((nkt, V_AUG, tq), BF16),
                        pltpu.VMEM((nkt // 2, V_AUG, 2 * tq), BF16),
                        pltpu.VMEM((nkt, tq, HD), BF16), pltpu.VMEM((nkt, V_AUG, tq), BF16)],
        compiler_params=pltpu.CompilerParams(dimension_semantics=("arbitrary",) * 3, vmem_limit_bytes=VMEM_LIMIT),
        name="nsa_prompt",
    )(nq, slc_t, slc_t, win_t, win_t, kcb, vcb, ngate, nsilu)


MEM_SLOTS = 2 * MEM_H


def _mem_kv_kernel(m_ref, g_ref, w_ref, o_ref):
    n_tok = m_ref.shape[1]
    res = _nn_dot(_rmsnorm_rows(m_ref[0], g_ref[...]).astype(BF16), w_ref[...])
    for c in range(MEM_SLOTS):
        o_ref[0, pl.ds(c, n_tok, stride=MEM_SLOTS), :] = res[:, c * MEM_HD:(c + 1) * MEM_HD]


def _mem_kv(mem, g_mem, w_bf):
    b, m, d = mem.shape
    return pl.pallas_call(
        _mem_kv_kernel, grid=(b,),
        in_specs=[pl.BlockSpec((1, m, d), lambda bi: (bi, 0, 0)), pl.BlockSpec((1, d), lambda bi: (0, 0)),
                  pl.BlockSpec((d, MEM_SLOTS * MEM_HD), lambda bi: (0, 0))],
        out_specs=pl.BlockSpec((1, m * MEM_SLOTS, MEM_HD), lambda bi: (bi, 0, 0)),
        out_shape=jax.ShapeDtypeStruct((b, m * MEM_SLOTS, MEM_HD), F32),
        compiler_params=pltpu.CompilerParams(dimension_semantics=("arbitrary",)),
        name="mem_kv",
    )(mem, g_mem, w_bf)


MEM_SCALE = MEM_HD ** -0.5


def _mem_attn_kernel(mq_ref, ms_ref, mkv_ref, o_ref):
    mq = mq_ref[0].astype(BF16)
    n_tok = mkv_ref.shape[1] // MEM_SLOTS
    outs = []
    for h in range(MEM_H):
        kh = mkv_ref[0, pl.ds(h, n_tok, stride=MEM_SLOTS), :].astype(BF16)
        vh = mkv_ref[0, pl.ds(MEM_H + h, n_tok, stride=MEM_SLOTS), :].astype(BF16)
        s = _nt_dot(mq[:, h * MEM_HD:(h + 1) * MEM_HD], kh) * MEM_SCALE
        e = jnp.exp(s - jnp.max(s, axis=1, keepdims=True))
        p = e / jnp.sum(e, axis=1, keepdims=True)
        outs.append(_nn_dot(p.astype(BF16), vh))
    o_ref[0] = (jnp.concatenate(outs, axis=1) * _silu(ms_ref[0].astype(F32))).astype(o_ref.dtype)


MERGE_TM = 512


def _mem_attn(mq, msilu, mkv):
    b, t, _ = mq.shape
    tm = min(MERGE_TM, t)
    rows = mkv.shape[1]
    row = lambda bi, ti: (bi, ti, 0)
    return pl.pallas_call(
        _mem_attn_kernel, grid=(b, t // tm),
        in_specs=[pl.BlockSpec((1, tm, BR_W), row), pl.BlockSpec((1, tm, BR_W), row),
                  pl.BlockSpec((1, rows, MEM_HD), lambda bi, ti: (bi, 0, 0))],
        out_specs=pl.BlockSpec((1, tm, BR_W), row),
        out_shape=jax.ShapeDtypeStruct((b, t, BR_W), BF16),
        compiler_params=pltpu.CompilerParams(dimension_semantics=("arbitrary", "arbitrary")),
        name="mem_attn",
    )(mq, msilu, mkv)


def _merge_kernel(an_ref, af_ref, am_ref, bg_ref, x_ref, wb_ref, wo_ref, gp_ref, o_ref):
    d = wo_ref.shape[0]
    merged = None
    for k, a_ref in enumerate((an_ref, af_ref, am_ref)):
        z = _nn_dot(a_ref[0].astype(BF16), wb_ref[k])
        term = _sigmoid(bg_ref[0, :, k * d:(k + 1) * d].astype(F32)) * z
        merged = term if merged is None else merged + term
    y = _nn_dot(merged.astype(BF16), wo_ref[...])
    o_ref[0] = x_ref[0] + _rmsnorm_rows(y, gp_ref[...])


def _merge(a_nsa, a_fox, a_mem, bgate, x, wb, wo, g_post):
    b, t, d = x.shape
    tm = min(MERGE_TM, t)
    row = lambda bi, ti: (bi, ti, 0)
    resident = dict(pipeline_mode=pl.Buffered(1))
    return pl.pallas_call(
        _merge_kernel, grid=(b, t // tm),
        in_specs=[pl.BlockSpec((1, tm, BR_W), row), pl.BlockSpec((1, tm, BR_W), row), pl.BlockSpec((1, tm, BR_W), row),
                  pl.BlockSpec((1, tm, 3 * d), row), pl.BlockSpec((1, tm, d), row),
                  pl.BlockSpec((3, BR_W, d), lambda bi, ti: (0, 0, 0), **resident),
                  pl.BlockSpec((d, d), lambda bi, ti: (0, 0), **resident),
                  pl.BlockSpec((1, d), lambda bi, ti: (0, 0))],
        out_specs=pl.BlockSpec((1, tm, d), row),
        out_shape=jax.ShapeDtypeStruct((b, t, d), F32),
        compiler_params=pltpu.CompilerParams(dimension_semantics=("arbitrary", "arbitrary"),
                                             vmem_limit_bytes=VMEM_LIMIT),
        name="merge",
    )(a_nsa, a_fox, a_mem, bgate, x, wb, wo, g_post)


_S_ORDER = ("nq", "ncmp", "nslc", "nwin", "nsilu", "fq", "fkv", "fsilu", "mq", "msilu", "bgate", "ngate", "flog")
_S_OFF = {}
_o = 0
for _k in _S_ORDER:
    _S_OFF[_k] = _o
    _o += _COLS[_k][1] - _COLS[_k][0]
assert _o == N_IN and _S_OFF["ngate"] % LANES == 0
N_PAST_SEL = N_SELECT - 1


def _row_to_col(row):
    n = row.shape[1]
    eye = lax.broadcasted_iota(jnp.int32, (n, n), 0) == lax.broadcasted_iota(jnp.int32, (n, n), 1)
    return jnp.sum(jnp.where(eye, row, 0.0), axis=1, keepdims=True)


def _col_to_row(col):
    n = col.shape[0]
    eye = lax.broadcasted_iota(jnp.int32, (n, n), 0) == lax.broadcasted_iota(jnp.int32, (n, n), 1)
    return jnp.sum(jnp.where(eye, col, 0.0), axis=0, keepdims=True)


def _hrow(hs_ref, key, off=0, n=None):
    lo = _S_OFF[key] + off
    n = (_COLS[key][1] - _COLS[key][0] - off) if n is None else n
    return hs_ref[0, :, lo:lo + n]


def _q_group(hs_ref, key, g):
    return jnp.concatenate([_hrow(hs_ref, key, (g * NSA_R + r) * HD, HD) for r in range(NSA_R)], axis=0) * SCALE


def _nsa_sample_sel_kernel(hs_ref, cb_ref, oc_ref, idx_ref):
    nblk = cb_ref.shape[3]
    idx_ref[...] = jnp.zeros(idx_ref.shape, idx_ref.dtype)
    for g in range(NSA_KV):
        q4 = _q_group(hs_ref, "nq", g).astype(BF16)
        s = _nt_dot(q4, cb_ref[0, 0, g].astype(BF16))
        e = jnp.exp(s - jnp.max(s, axis=1, keepdims=True))
        p = e / jnp.sum(e, axis=1, keepdims=True)
        oc_ref[0, g * NSA_R:(g + 1) * NSA_R, :] = _nn_dot(p.astype(BF16), cb_ref[0, 1, g].astype(BF16))
        imp = jnp.sum(p, axis=0, keepdims=True)
        col = _row_to_col(imp)
        mi = lax.broadcasted_iota(jnp.int32, (nblk, nblk), 0)
        ni = lax.broadcasted_iota(jnp.int32, (nblk, nblk), 1)
        ahead = (col > imp) | ((col == imp) & (mi < ni))
        rank = jnp.sum(jnp.where(ahead, 1.0, 0.0), axis=0, keepdims=True)
        sel = jnp.where(rank < N_PAST_SEL, 1.0, 0.0)
        pos, carry = [], jnp.zeros((1, 1), F32)
        for ch in range(nblk // LANES):
            chunk = sel[:, ch * LANES:(ch + 1) * LANES]
            incl = _lane_cumsum(chunk) + carry
            carry = incl[:, LANES - 1:LANES]
            pos.append(incl - chunk)
        pos = jnp.concatenate(pos, axis=1)
        slot = lax.broadcasted_iota(jnp.int32, (N_SELECT, nblk), 0).astype(F32)
        onehot = jnp.where((pos == slot) & (sel > 0.5), 1.0, 0.0).astype(BF16)
        ids = lax.broadcasted_iota(jnp.int32, (1, nblk), 1).astype(F32).astype(BF16)
        idx = _nt_dot(ids, onehot)
        idx_ref[0, g:g + 1, 0:N_SELECT] = idx.astype(jnp.int32)


def _nsa_sample_sel(hs3, cb):
    n = hs3.shape[0]
    nblk = cb.shape[3]
    assert nblk % LANES == 0 and nblk >= N_PAST_SEL and nblk <= 256
    return pl.pallas_call(
        _nsa_sample_sel_kernel, grid=(n,),
        in_specs=[pl.BlockSpec((1, 1, N_IN_PAD), lambda b: (b, 0, 0)),
                  pl.BlockSpec((1, 2, NSA_KV, nblk, HD), lambda b: (b, 0, 0, 0, 0))],
        out_specs=[pl.BlockSpec((1, NSA_H, HD), lambda b: (b, 0, 0)),
                   pl.BlockSpec((1, NSA_KV, LANES), lambda b: (b, 0, 0))],
        out_shape=[jax.ShapeDtypeStruct((n, NSA_H, HD), F32), jax.ShapeDtypeStruct((n, NSA_KV, LANES), jnp.int32)],
        compiler_params=pltpu.CompilerParams(dimension_semantics=("arbitrary",)),
        name="nsa_sample_select",
    )(hs3, cb)


def _nsa_sample_attn_kernel(idx_ref, pt_ref, hs_ref, oc_ref, win_ref, *refs):
    n_blk = NSA_KV * N_PAST_SEL
    blk_refs, (o_ref, nw_ref) = refs[:n_blk], refs[n_blk:]
    b = pl.program_id(0)
    gates = _sigmoid(_hrow(hs_ref, "ngate"))
    lane_half = lax.broadcasted_iota(jnp.int32, (NSA_R, PAGE_SIZE), 1) // CMP_BLOCK
    outs = []
    for g in range(NSA_KV):
        q4 = _q_group(hs_ref, "nq", g)
        q4b = q4.astype(BF16)
        kn, vn = _hrow(hs_ref, "nslc", g * HD, HD), _hrow(hs_ref, "nslc", KV_W + g * HD, HD)
        s_cur = jnp.sum(q4 * kn, axis=1, keepdims=True)
        s_list, masks = [], []
        for k in range(N_PAST_SEL):
            half = idx_ref[b, g * LANES + k] & 1
            mask = lane_half == half
            s = _nn_dot(q4b, blk_refs[g * N_PAST_SEL + k][0, 0, 0].astype(BF16))
            s_list.append(jnp.where(mask, s, NEG))
            masks.append(mask)
        m = s_cur
        for s in s_list:
            m = jnp.maximum(m, jnp.max(s, axis=1, keepdims=True))
        l = jnp.exp(s_cur - m)
        acc = l * vn
        for k in range(N_PAST_SEL):
            e = jnp.where(masks[k], jnp.exp(s_list[k] - m), 0.0)
            l = l + jnp.sum(e, axis=1, keepdims=True)
            acc = acc + _nt_dot(e.astype(BF16), blk_refs[g * N_PAST_SEL + k][0, 1, 0].astype(BF16))
        o_s = acc / jnp.maximum(l, TINY)
        kwn, vwn = _hrow(hs_ref, "nwin", g * HD, HD), _hrow(hs_ref, "nwin", KV_W + g * HD, HD)
        s_w = _nn_dot(q4b, win_ref[0, 0, g].astype(BF16))
        s_wn = jnp.sum(q4 * kwn, axis=1, keepdims=True)
        m = jnp.maximum(jnp.max(s_w, axis=1, keepdims=True), s_wn)
        e, en = jnp.exp(s_w - m), jnp.exp(s_wn - m)
        l = jnp.sum(e, axis=1, keepdims=True) + en
        o_w = (_nt_dot(e.astype(BF16), win_ref[0, 1, g].astype(BF16)) + en * vwn) / jnp.maximum(l, TINY)
        o_c = oc_ref[0, g * NSA_R:(g + 1) * NSA_R, :]
        for r in range(NSA_R):
            c = g * NSA_R + r
            gate = lambda br: gates[:, br * NSA_H + c: br * NSA_H + c + 1]
            outs.append(gate(0) * o_c[r:r + 1] + gate(1) * o_s[r:r + 1] + gate(2) * o_w[r:r + 1])
    o_ref[0] = jnp.concatenate(outs, axis=1) * _silu(_hrow(hs_ref, "nsilu"))
    wb = win_ref.shape[4]
    newcol = _row_to_col(_hrow(hs_ref, "nwin"))
    last = lax.broadcasted_iota(jnp.int32, (HD, wb), 1) == wb - 1
    for kv in range(2):
        for g in range(NSA_KV):
            r0 = (kv * NSA_KV + g) * HD
            shifted = pltpu.roll(win_ref[0, kv, g], wb - 1, axis=1)
            nw_ref[0, kv, g] = jnp.where(last, newcol[r0:r0 + HD], shifted)


def _nsa_sample_attn(idx, page_table, hs3, o_c, win_state, pool_slc):
    n = hs3.shape[0]
    wb = win_state.shape[4]
    assert wb == WINDOW
    blk_specs = [pl.BlockSpec((1, 2, 1, HD, PAGE_SIZE),
                              lambda b, idx, pt, g=g, k=k: (pt[b, idx[b, g * LANES + k] >> 1], 0, g, 0, 0))
                 for g in range(NSA_KV) for k in range(N_PAST_SEL)]
    grid_spec = pltpu.PrefetchScalarGridSpec(
        num_scalar_prefetch=2, grid=(n,),
        in_specs=[pl.BlockSpec((1, 1, N_IN_PAD), lambda b, idx, pt: (b, 0, 0)),
                  pl.BlockSpec((1, NSA_H, HD), lambda b, idx, pt: (b, 0, 0)),
                  pl.BlockSpec((1, 2, NSA_KV, HD, wb), lambda b, idx, pt: (b, 0, 0, 0, 0))] + blk_specs,
        out_specs=[pl.BlockSpec((1, 1, BR_W), lambda b, idx, pt: (b, 0, 0)),
                   pl.BlockSpec((1, 2, NSA_KV, HD, wb), lambda b, idx, pt: (b, 0, 0, 0, 0))],
    )
    return pl.pallas_call(
        _nsa_sample_attn_kernel, grid_spec=grid_spec,
        out_shape=[jax.ShapeDtypeStruct((n, 1, BR_W), F32), jax.ShapeDtypeStruct((n, 2, NSA_KV, HD, wb), F32)],
        compiler_params=pltpu.CompilerParams(dimension_semantics=("arbitrary",)),
        name="nsa_sample_attn",
    )(idx, page_table, hs3, o_c, win_state, *([pool_slc] * (NSA_KV * N_PAST_SEL)))


FOX_NPG = 16


def _fox_sample_kernel(pt_ref, hs_ref, *refs):
    kv_refs = refs[:FOX_NPG]
    lf_hbm, o_ref, q_b, acc, m_s, l_s, r_s, lf_buf, lf_sem = refs[FOX_NPG:]
    n_pages = r_s.shape[0]
    b, j = pl.program_id(0), pl.program_id(1)

    def lf_copy(b_, p, slot_):
        return pltpu.make_async_copy(lf_hbm.at[pt_ref[b_, p]], lf_buf.at[slot_, p], lf_sem.at[slot_])

    @pl.when(j == 0)
    def _():
        slot = b % 2

        @pl.when(b == 0)
        def _():
            for p in range(n_pages):
                lf_copy(b, p, slot).start()

        @pl.when(b + 1 < pl.num_programs(0))
        def _():
            for p in range(n_pages):
                lf_copy(b + 1, p, 1 - slot).start()

        for h in range(FOX_H):
            q_b[h] = jnp.broadcast_to(_row_to_col(_hrow(hs_ref, "fq", h * HD, HD) * SCALE), (HD, PAGE_SIZE))
        acc[...] = jnp.zeros(acc.shape, F32)
        m_s[...] = jnp.full(m_s.shape, NEG, F32)
        l_s[...] = jnp.zeros(l_s.shape, F32)
        for p in range(n_pages):
            lf_copy(0, p, slot).wait()
        incl = _lane_cumsum(lf_buf[slot].reshape(n_pages * FOX_H, PAGE_SIZE))
        later = _row_to_col(_hrow(hs_ref, "flog"))
        for p in reversed(range(n_pages)):
            inc = incl[p * FOX_H:(p + 1) * FOX_H]
            total = inc[:, PAGE_SIZE - 1:PAGE_SIZE]
            r_s[p] = later + (total - inc)
            later = later + total

    s_list = []
    for i in range(FOX_NPG):
        qk = jnp.concatenate([jnp.sum(kv_refs[i][0, 0, h] * q_b[h], axis=0, keepdims=True) for h in range(FOX_H)], axis=0)
        s_list.append(qk + r_s[j * FOX_NPG + i])
    s_max = s_list[0]
    for s in s_list[1:]:
        s_max = jnp.maximum(s_max, s)
    m_old = m_s[...]
    m_new = jnp.maximum(m_old, jnp.max(s_max, axis=1, keepdims=True))
    alpha = jnp.exp(m_old - m_new)
    p_list = [jnp.exp(s - m_new) for s in s_list]
    p_sum = p_list[0]
    for p in p_list[1:]:
        p_sum = p_sum + p
    m_s[...] = m_new
    l_s[...] = l_s[...] * alpha + jnp.sum(p_sum, axis=1, keepdims=True)
    for h in range(FOX_H):
        a = acc[h] * alpha[h:h + 1, :]
        for i in range(FOX_NPG):
            a = a + kv_refs[i][0, 1, h] * p_list[i][h:h + 1, :]
        acc[h] = a

    @pl.when(j == pl.num_programs(1) - 1)
    def _():
        outs = []
        for h in range(FOX_H):
            q = _hrow(hs_ref, "fq", h * HD, HD) * SCALE
            kn, vn = _hrow(hs_ref, "fkv", h * HD, HD), _hrow(hs_ref, "fkv", FOX_H * HD + h * HD, HD)
            s_n = jnp.sum(q * kn, axis=1, keepdims=True)
            m_h, l_h = m_s[h:h + 1, :], l_s[h:h + 1, :]
            m_f = jnp.maximum(m_h, s_n)
            al, pn = jnp.exp(m_h - m_f), jnp.exp(s_n - m_f)
            past = _col_to_row(jnp.sum(acc[h], axis=1, keepdims=True))
            outs.append((past * al + pn * vn) / jnp.maximum(l_h * al + pn, TINY))
        o_ref[0] = jnp.concatenate(outs, axis=1) * _silu(_hrow(hs_ref, "fsilu"))


def _fox_sample(page_table, hs3, pool_kv, pool_lf):
    n, n_pages = page_table.shape
    assert n_pages % FOX_NPG == 0
    ng = n_pages // FOX_NPG
    kv_specs = [pl.BlockSpec((1, 2, FOX_H, HD, PAGE_SIZE), lambda b, j, pt, i=i: (pt[b, j * FOX_NPG + i], 0, 0, 0, 0))
                for i in range(FOX_NPG)]
    grid_spec = pltpu.PrefetchScalarGridSpec(
        num_scalar_prefetch=1, grid=(n, ng),
        in_specs=[pl.BlockSpec((1, 1, N_IN_PAD), lambda b, j, pt: (b, 0, 0))] + kv_specs
                 + [pl.BlockSpec(memory_space=pl.ANY)],
        out_specs=pl.BlockSpec((1, 1, BR_W), lambda b, j, pt: (b, 0, 0)),
        scratch_shapes=[pltpu.VMEM((FOX_H, HD, PAGE_SIZE), F32), pltpu.VMEM((FOX_H, HD, PAGE_SIZE), F32),
                        pltpu.VMEM((FOX_H, 1), F32), pltpu.VMEM((FOX_H, 1), F32),
                        pltpu.VMEM((n_pages, FOX_H, PAGE_SIZE), F32),
                        pltpu.VMEM((2, n_pages, FOX_H, PAGE_SIZE), F32), pltpu.SemaphoreType.DMA((2,))],
    )
    return pl.pallas_call(
        _fox_sample_kernel, grid_spec=grid_spec,
        out_shape=jax.ShapeDtypeStruct((n, 1, BR_W), F32),
        compiler_params=pltpu.CompilerParams(dimension_semantics=("arbitrary", "arbitrary"),
                                             vmem_limit_bytes=VMEM_LIMIT),
        name="fox_sample",
    )(page_table, hs3, *([pool_kv] * FOX_NPG), pool_lf)


def _split_weights(w_in):
    wt = w_in.T
    sl = lambda k: wt[_COLS[k][0]:_COLS[k][1]]
    d = wt.shape[1]
    wq = jnp.concatenate([sl(k) for k in _QGROUPS] + [sl("bgate"), sl("ngate"),
                                                        jnp.zeros((LANES - 24, d), wt.dtype)], axis=0).astype(BF16)
    wkv = jnp.concatenate([sl(k) for k, _ in _KVGROUPS], axis=0).astype(BF16)
    w_all = jnp.concatenate([sl(k) for k in _S_ORDER] + [jnp.zeros((N_IN_PAD - N_IN, d), wt.dtype)],
                            axis=0).astype(BF16)
    return wq, wkv, w_all


def kernel(x_prompt, x_sample, cache_nsa_cmp, cache_nsa_slc, state_nsa_win, cache_fox_kv, cache_fox_logf, cache_mem_kv, page_table, mem_prompt, g_pre, g_post, g_mem, w_in, b_fgate, w_mem_kv, w_cmp1, w_cmp2, pe_cmp, w_branch, w_out):
    assert w_in.shape[0] == 1 and x_sample.shape[1] == 1, "one layer, one new token per sequence"
    b, t, d = x_prompt.shape
    n_s, n_pages = page_table.shape
    past = n_pages * PAGE_SIZE
    wq, wkv, w_all = _split_weights(w_in[0])
    w1bd, w2bd, pe_t = _compress_weights(w_cmp1[0], w_cmp2[0], pe_cmp[0])
    wb, wo = w_branch[0].astype(BF16), w_out[0].astype(BF16)

    (nq, nsilu, fq, fsilu, mq, msilu, bgate, ngate, cmp_t, slc_t, win_t, fkv_t, flog_t) = _proj_prompt(
        x_prompt, g_pre, wq, wkv, _rope_tables(jnp.arange(t, dtype=F32)), b_fgate[0])
    cb_p = _compress(cmp_t.reshape(b, 2, NSA_KV, HD, t), jnp.zeros((1, 1), jnp.int32), w1bd, w2bd, pe_t,
                     n_batch=b, n_pages=t // PAGE_SIZE, paged=False)
    cb_p = cb_p.reshape(b, 2, NSA_KV, t // CMP_BLOCK, HD)
    a_nsa = _nsa_prompt(nq, slc_t, win_t, cb_p, cb_p, ngate, nsilu)
    a_fox = _fox_prompt(fq, fkv_t, flog_t, fsilu)
    mkv_p = _mem_kv(mem_prompt, g_mem, w_mem_kv[0].astype(BF16))
    a_mem = _mem_attn(mq, msilu, mkv_p)
    y_p = _merge(a_nsa, a_fox, a_mem, bgate, x_prompt, wb, wo, g_post)

    rows_minor = lambda a: jnp.transpose(a, (0, 2, 3, 4, 1))
    hs = _proj_sample(x_sample[:, 0, :], g_pre, w_all, _rope_tables(jnp.full((n_s,), past, F32)), b_fgate[0])
    hs3 = hs.reshape(n_s, 1, N_IN_PAD)
    cb_s = _compress(rows_minor(cache_nsa_cmp[0]), page_table, w1bd, w2bd, pe_t,
                     n_batch=n_s, n_pages=n_pages, paged=True)
    cb_s = cb_s.reshape(n_s, 2, NSA_KV, past // CMP_BLOCK, HD)
    o_c, idx = _nsa_sample_sel(hs3, cb_s)
    a_nsa_s, new_win = _nsa_sample_attn(idx.reshape(n_s, NSA_KV * LANES), page_table, hs3, o_c,
                                        rows_minor(state_nsa_win[0]), rows_minor(cache_nsa_slc[0]))
    a_fox_s = _fox_sample(page_table, hs3, rows_minor(cache_fox_kv[0]),
                          jnp.transpose(cache_fox_logf[0], (0, 2, 1)))
    col = lambda k: hs[:, _S_OFF[k]:_S_OFF[k] + _COLS[k][1] - _COLS[k][0]]
    a_mem_s = _mem_attn(col("mq").reshape(n_s, 1, BR_W), col("msilu").reshape(n_s, 1, BR_W),
                        cache_mem_kv[0].reshape(n_s, -1, MEM_HD))
    y_s = _merge(a_nsa_s.reshape(1, n_s, BR_W), a_fox_s.reshape(1, n_s, BR_W), a_mem_s.reshape(1, n_s, BR_W),
                 col("bgate").reshape(1, n_s, 3 * d), x_sample.reshape(1, n_s, d), wb, wo, g_post)

    def rows_major(a_t, heads):
        n, _, r = a_t.shape
        return jnp.transpose(a_t.reshape(n, 2, heads, HD, r), (0, 4, 1, 2, 3))[None]

    wkeep = min(WINDOW, t)
    return (y_p, y_s.reshape(n_s, 1, d),
            rows_major(cmp_t, NSA_KV), rows_major(slc_t, NSA_KV), rows_major(win_t[:, :, t - wkeep:], NSA_KV),
            rows_major(fkv_t, FOX_H), jnp.transpose(flog_t, (0, 2, 1))[None],
            mkv_p.reshape(1, b, -1, 2, MEM_H, MEM_HD),
            col("ncmp").reshape(1, n_s, 1, 2, NSA_KV, HD), col("nslc").reshape(1, n_s, 1, 2, NSA_KV, HD),
            jnp.transpose(new_win, (0, 4, 1, 2, 3))[None],
            col("fkv").reshape(1, n_s, 1, 2, FOX_H, HD), col("flog").reshape(1, n_s, 1, FOX_H))
```

```python
import functools

import numpy as np
import jax
import jax.numpy as jnp
from jax import lax
from jax.experimental import pallas as pl
from jax.experimental.pallas import tpu as pltpu

HD = 64
NSA_H = 8
NSA_KV = 2
NSA_R = NSA_H // NSA_KV
CMP_BLOCK = 64
N_SELECT = 16
WINDOW = 512
FOX_H = 8
MEM_H = 4
MEM_HD = 128
BR_W = 512
ROT_DIM = HD // 4
ROT_HALF = ROT_DIM // 2
ROPE_THETA = 500000.0
PAGE_SIZE = 128
EPS = 1e-6
NEG = -1e30
BIG = 1e4
TINY = 1e-30
KV_W = NSA_KV * HD
SCALE = HD ** -0.5

LANES = 128
SUBLANES = 8
VMEM_LIMIT = 56 * 1024 * 1024

_COLS = dict(nq=(0, 512), ncmp=(512, 768), nslc=(768, 1024), nwin=(1024, 1280), ngate=(1280, 1304),
             nsilu=(1304, 1816), fq=(1816, 2328), fkv=(2328, 3352), flog=(3352, 3360), fsilu=(3360, 3872),
             mq=(3872, 4384), msilu=(4384, 4896), bgate=(4896, 7968))
N_IN = 7968
N_IN_PAD = 8064

F32 = jnp.float32
BF16 = jnp.bfloat16


def _nt_dot(a, b):
    return lax.dot_general(a, b, (((1,), (1,)), ((), ())), preferred_element_type=F32)


def _nn_dot(a, b):
    return lax.dot_general(a, b, (((1,), (0,)), ((), ())), preferred_element_type=F32)


def _rmsnorm_rows(x, g):
    var = jnp.mean(x * x, axis=-1, keepdims=True)
    return x * lax.rsqrt(var + EPS) * g


def _silu(x):
    return x * (1.0 / (1.0 + jnp.exp(-x)))


def _sigmoid(x):
    return 1.0 / (1.0 + jnp.exp(-x))


def _rope_tables(pos):
    inv_freq = ROPE_THETA ** (-jnp.arange(ROT_HALF, dtype=F32) / ROT_HALF)
    ang = pos[:, None] * inv_freq[None, :]
    cos, sin = jnp.cos(ang), jnp.sin(ang)
    t = pos.shape[0]
    one = jnp.ones((t, HD - ROT_DIM), F32)
    zero = jnp.zeros((t, HD - ROT_HALF), F32)
    c_head = jnp.concatenate([cos, cos, one], axis=1)
    s_hi = jnp.concatenate([-sin, zero], axis=1)
    s_lo = jnp.concatenate([jnp.zeros((t, ROT_HALF), F32), sin, jnp.zeros((t, HD - ROT_DIM), F32)], axis=1)
    tile2 = lambda a: jnp.concatenate([a, a], axis=1)
    return tile2(c_head), tile2(s_hi), tile2(s_lo), cos, sin


def _rope_lanes(x, c, s_hi, s_lo):
    k = x.shape[1] // LANES
    if k > 1:
        c = jnp.concatenate([c] * k, axis=1)
        s_hi = jnp.concatenate([s_hi] * k, axis=1)
        s_lo = jnp.concatenate([s_lo] * k, axis=1)
    n = x.shape[1]
    up = pltpu.roll(x, n - ROT_HALF, axis=1)
    dn = pltpu.roll(x, ROT_HALF, axis=1)
    return x * c + up * s_hi + dn * s_lo


_QGROUPS = ("nq", "nsilu", "fq", "fsilu", "mq", "msilu")
_KVGROUPS = (("ncmp", 256), ("nslc", 256), ("nwin", 256), ("fkv", 1024), ("flog", 8))
NQ_ROWS = 6 * 512 + 3072 + LANES
NKV_ROWS = 256 * 3 + 1024 + 8
PROJ_TM = 512


def _proj_prompt_kernel(x_ref, g_ref, wq_ref, wkv_ref, c_ref, shi_ref, slo_ref, cosT_ref, sinT_ref, bf_ref,
                        nq_ref, nsilu_ref, fq_ref, fsilu_ref, mq_ref, msilu_ref, bgate_ref, ngate_ref,
                        cmpT_ref, slcT_ref, winT_ref, fkvT_ref, flogT_ref):
    xb = _rmsnorm_rows(x_ref[0], g_ref[...]).astype(BF16)
    outs = (nq_ref, nsilu_ref, fq_ref, fsilu_ref, mq_ref, msilu_ref)
    for i, o_ref in enumerate(outs):
        h = _nt_dot(xb, wq_ref[i * 512:(i + 1) * 512, :])
        if i == 0:
            h = _rope_lanes(h, c_ref[...], shi_ref[...], slo_ref[...])
        if i in (0, 2):
            h = h * (SCALE * LOG2E)
        o_ref[0] = h.astype(o_ref.dtype)
    off = 6 * 512
    for j in range(3072 // 512):
        h = _nt_dot(xb, wq_ref[off + j * 512: off + (j + 1) * 512, :])
        bgate_ref[0, :, j * 512:(j + 1) * 512] = h.astype(bgate_ref.dtype)
    off += 3072
    ngate_ref[0] = _nt_dot(xb, wq_ref[off:off + LANES, :])
    cosT, sinT = cosT_ref[...], sinT_ref[...]
    roff = 0
    for o_ref in (cmpT_ref, slcT_ref, winT_ref):
        hT = _nt_dot(wkv_ref[roff:roff + 256, :], xb)
        o_ref[0] = hT
        for g in range(NSA_KV):
            x1 = hT[g * HD: g * HD + ROT_HALF, :]
            x2 = hT[g * HD + ROT_HALF: g * HD + ROT_DIM, :]
            o_ref[0, g * HD: g * HD + ROT_HALF, :] = x1 * cosT - x2 * sinT
            o_ref[0, g * HD + ROT_HALF: g * HD + ROT_DIM, :] = x2 * cosT + x1 * sinT
        roff += 256
    for j in range(1024 // 256):
        fkvT_ref[0, j * 256:(j + 1) * 256, :] = _nt_dot(wkv_ref[roff + j * 256: roff + (j + 1) * 256, :], xb)
    roff += 1024
    flogT_ref[0] = _log_sigmoid(_nt_dot(wkv_ref[roff:roff + 8, :], xb) + bf_ref[...])


def _proj_prompt(x, g_pre, wq, wkv, tabs, b_fgate):
    b, t, d = x.shape
    tm = min(PROJ_TM, t)
    nt = t // tm
    c, shi, slo, cos, sin = tabs
    cosT, sinT = cos.T, sin.T
    row = lambda bi, ti: (bi, ti, 0)
    col = lambda bi, ti: (bi, 0, ti)
    const = lambda bi, ti: (0, 0)
    tab = lambda bi, ti: (ti, 0)
    tabT = lambda bi, ti: (0, ti)
    resident = dict(pipeline_mode=pl.Buffered(1))
    in_specs = [
        pl.BlockSpec((1, tm, d), row),
        pl.BlockSpec((1, d), const),
        pl.BlockSpec((NQ_ROWS, d), const, **resident),
        pl.BlockSpec((NKV_ROWS, d), const, **resident),
        pl.BlockSpec((tm, LANES), tab), pl.BlockSpec((tm, LANES), tab), pl.BlockSpec((tm, LANES), tab),
        pl.BlockSpec((ROT_HALF, tm), tabT), pl.BlockSpec((ROT_HALF, tm), tabT),
        pl.BlockSpec((FOX_H, 1), const),
    ]
    q_shape = jax.ShapeDtypeStruct((b, t, 512), BF16)
    out_shape = [q_shape] * 6 + [
        jax.ShapeDtypeStruct((b, t, 3072), BF16),
        jax.ShapeDtypeStruct((b, t, LANES), F32),
        jax.ShapeDtypeStruct((b, 256, t), F32), jax.ShapeDtypeStruct((b, 256, t), F32),
        jax.ShapeDtypeStruct((b, 256, t), F32),
        jax.ShapeDtypeStruct((b, 1024, t), F32),
        jax.ShapeDtypeStruct((b, 8, t), F32),
    ]
    out_specs = [pl.BlockSpec((1, tm, 512), row)] * 6 + [
        pl.BlockSpec((1, tm, 3072), row),
        pl.BlockSpec((1, tm, LANES), row),
        pl.BlockSpec((1, 256, tm), col), pl.BlockSpec((1, 256, tm), col), pl.BlockSpec((1, 256, tm), col),
        pl.BlockSpec((1, 1024, tm), col),
        pl.BlockSpec((1, 8, tm), col),
    ]
    return pl.pallas_call(
        _proj_prompt_kernel,
        grid=(b, nt),
        in_specs=in_specs,
        out_specs=out_specs,
        out_shape=out_shape,
        compiler_params=pltpu.CompilerParams(dimension_semantics=("arbitrary", "arbitrary"),
                                             vmem_limit_bytes=VMEM_LIMIT),
        name="proj_prompt",
    )(x, g_pre, wq, wkv, c, shi, slo, cosT, sinT, b_fgate.reshape(FOX_H, 1))


PROJ_S_TN = 1152


def _proj_sample_kernel(x_ref, g_ref, w_ref, c_ref, shi_ref, slo_ref, bf_ref, o_ref):
    xb = _rmsnorm_rows(x_ref[...], g_ref[...]).astype(BF16)
    o_ref[...] = _nt_dot(xb, w_ref[...])

    @pl.when(pl.program_id(0) == _S_OFF["flog"] // PROJ_S_TN)
    def _():
        lo = _S_OFF["flog"] % PROJ_S_TN
        o_ref[:, lo:lo + FOX_H] = _log_sigmoid(o_ref[:, lo:lo + FOX_H] + bf_ref[...])

    @pl.when(pl.program_id(0) == 0)
    def _():
        c, shi, slo = c_ref[...], shi_ref[...], slo_ref[...]
        for c0 in (0, 128, 256, 384, 512, 768, 1024):
            o_ref[:, c0:c0 + LANES] = _rope_lanes(o_ref[:, c0:c0 + LANES], c, shi, slo)


def _proj_sample(x, g_pre, w_all, tabs, b_fgate):
    n, d = x.shape
    c, shi, slo, _, _ = tabs
    const = lambda j: (0, 0)
    return pl.pallas_call(
        _proj_sample_kernel,
        grid=(N_IN_PAD // PROJ_S_TN,),
        in_specs=[pl.BlockSpec((n, d), const), pl.BlockSpec((1, d), const),
                  pl.BlockSpec((PROJ_S_TN, d), lambda j: (j, 0)),
                  pl.BlockSpec((n, LANES), const), pl.BlockSpec((n, LANES), const), pl.BlockSpec((n, LANES), const),
                  pl.BlockSpec((1, FOX_H), const)],
        out_specs=pl.BlockSpec((n, PROJ_S_TN), lambda j: (0, j)),
        out_shape=jax.ShapeDtypeStruct((n, N_IN_PAD), F32),
        compiler_params=pltpu.CompilerParams(dimension_semantics=("arbitrary",)),
        name="proj_sample",
    )(x, g_pre, w_all, c, shi, slo, b_fgate.reshape(1, FOX_H))


CMP_PITCH = 72
CMP_DCHUNK = 8


def _compress_kernel(pt_ref, src_ref, w1_ref, w2_ref, pe_ref, o_ref, buf, sem, *, n_pages, paged):
    bi, kv = pl.program_id(0), pl.program_id(1)
    n_steps = pl.num_programs(0) * 2
    step = bi * 2 + kv
    slot = step % 2

    def page_copy(b_, kv_, i, slot_):
        if paged:
            src = src_ref.at[pt_ref[b_, i], kv_]
        else:
            src = src_ref.at[b_, kv_, :, :, pl.ds(i * PAGE_SIZE, PAGE_SIZE)]
        dst = buf.at[slot_, :, pl.ds(i * CMP_PITCH, HD), :]
        return pltpu.make_async_copy(src, dst, sem.at[slot_])

    def issue(b_, kv_, slot_):
        for i in range(n_pages):
            page_copy(b_, kv_, i, slot_).start(priority=i % 2)

    @pl.when(step == 0)
    def _():
        issue(bi, kv, slot)

    @pl.when(step + 1 < n_steps)
    def _():
        nxt = step + 1
        issue(nxt // 2, nxt % 2, 1 - slot)

    for i in range(n_pages):
        page_copy(0, 0, i, slot).wait()

    acc = jnp.zeros((2 * n_pages, LANES), F32)
    for c in range(HD // CMP_DCHUNK):
        cols = []
        for dd in range(CMP_DCHUNK):
            d = c * CMP_DCHUNK + dd
            rows = [buf[slot, g, pl.ds(d, n_pages, stride=CMP_PITCH), :] for g in range(NSA_KV)]
            x = jnp.concatenate(rows, axis=0) + pe_ref[0, d:d + 1, :]
            cols.append(x.astype(BF16))
        lhs = jnp.concatenate(cols, axis=1)
        k0 = c * CMP_DCHUNK * LANES
        acc = acc + _nn_dot(lhs, w1_ref[0, k0:k0 + CMP_DCHUNK * LANES, :])
    hid = _silu(acc)
    out = _nn_dot(hid.astype(BF16), w2_ref[0])
    o_ref[0, 0, 0] = out[:n_pages]
    o_ref[0, 0, 1] = out[n_pages:]


def _compress(src, page_table, w1bd, w2bd, pe_t, *, n_batch, n_pages, paged):
    kern = functools.partial(_compress_kernel, n_pages=n_pages, paged=paged)
    grid_spec = pltpu.PrefetchScalarGridSpec(
        num_scalar_prefetch=1,
        grid=(n_batch, 2),
        in_specs=[pl.BlockSpec(memory_space=pl.ANY),
                  pl.BlockSpec((1, HD * LANES, LANES), lambda b, kv, pt: (kv, 0, 0)),
                  pl.BlockSpec((1, LANES, LANES), lambda b, kv, pt: (kv, 0, 0)),
                  pl.BlockSpec((1, HD, LANES), lambda b, kv, pt: (kv, 0, 0))],
        out_specs=pl.BlockSpec((1, 1, NSA_KV, n_pages, LANES), lambda b, kv, pt: (b, kv, 0, 0, 0)),
        scratch_shapes=[pltpu.VMEM((2, NSA_KV, n_pages * CMP_PITCH, LANES), F32),
                        pltpu.SemaphoreType.DMA((2,))],
    )
    return pl.pallas_call(
        kern, grid_spec=grid_spec,
        out_shape=jax.ShapeDtypeStruct((n_batch, 2, NSA_KV, n_pages, LANES), F32),
        compiler_params=pltpu.CompilerParams(dimension_semantics=("arbitrary", "arbitrary"),
                                             vmem_limit_bytes=VMEM_LIMIT),
        name="compress_paged" if paged else "compress_prompt",
    )(page_table, src, w1bd, w2bd, pe_t)


def _compress_weights(w1, w2, pe):
    eye2 = jnp.eye(2, dtype=w1.dtype)
    base = jnp.transpose(w1, (0, 2, 1, 3)).astype(BF16)
    zero = jnp.zeros_like(base)
    w1bd = jnp.stack([jnp.concatenate([base, zero], axis=-1), jnp.concatenate([zero, base], axis=-1)], axis=2)
    w1bd = w1bd.reshape(2, HD * LANES, LANES)
    w2bd = jnp.einsum('kef,nm->knemf', w2, eye2).reshape(2, LANES, LANES).astype(BF16)
    pe_t = jnp.tile(jnp.transpose(pe, (0, 2, 1)), (1, 1, 2))
    return w1bd, w2bd, pe_t


def _log_sigmoid(x):
    return jnp.minimum(x, 0.0) - jnp.log1p(jnp.exp(-jnp.abs(x)))


def _lane_cumsum(x):
    lane = lax.broadcasted_iota(jnp.int32, x.shape, 1)
    s = 1
    while s < LANES:
        x = x + jnp.where(lane >= s, pltpu.roll(x, s, axis=1), 0.0)
        s *= 2
    return x


M_INIT = -1e29


def _online_update(carry, s, v_t):
    m, acc = carry
    m_new = jnp.maximum(m, jnp.max(s, axis=0, keepdims=True))
    alpha = jnp.exp2(m - m_new)
    pb = jnp.exp2(s - m_new).astype(BF16)
    if isinstance(v_t, (list, tuple)):
        w = pb.shape[1] // len(v_t)
        pv = jnp.concatenate([_nn_dot(v, pb[:, j * w:(j + 1) * w]) for j, v in enumerate(v_t)], axis=1)
    else:
        pv = _nn_dot(v_t, pb)
    return m_new, acc * alpha + pv


V_AUG = HD + 16
LOG2E = 1.4426950408889634


def _online_init(nq):
    return (jnp.full((1, nq), M_INIT, F32), jnp.zeros((V_AUG, nq), F32))


def _augment_values(v):
    n = v.shape[1]
    tail = jnp.where(lax.broadcasted_iota(jnp.int32, (V_AUG - HD, n), 0) == 0, 1.0, 0.0)
    return jnp.concatenate([v, tail], axis=0).astype(BF16)


def _mask_bias(cond):
    return jnp.where(cond, 0.0, NEG)


def _online_finish(carry):
    _, acc = carry
    return acc[:HD] / jnp.maximum(acc[HD:HD + 1], TINY)


ATT_T = 256
FOX_HPS = 4


def _fox_prompt_kernel(q_ref, k_ref, v_ref, logf_ref, gate_ref, o_ref, k_r, v_t, v_t2, c_b):
    hg, qi = pl.program_id(1), pl.program_id(2)
    tk = ATT_T
    tq = ATT_T
    t = k_ref.shape[2]
    nkt = t // tk
    nh = FOX_HPS

    @pl.when(qi == 0)
    def _():
        for k2 in range(nkt // 2):
            sl = slice(2 * k2 * tk, 2 * (k2 + 1) * tk)
            k_r[k2] = k_ref[0, :, sl].T.astype(BF16)
            v = jnp.concatenate([_augment_values(v_ref[0, j * HD:(j + 1) * HD, sl]) for j in range(nh)], axis=0)
            v_t2[k2] = v
            v_t[2 * k2] = v[:, :tk]
            v_t[2 * k2 + 1] = v[:, tk:]
        carry = jnp.zeros((FOX_H, 1), F32)
        per = 2 * tk // LANES
        for ch in range(t // LANES):
            x = _lane_cumsum(logf_ref[0, :, ch * LANES:(ch + 1) * LANES]) + carry
            carry = x[:, LANES - 1:LANES]
            head = lax.broadcasted_iota(jnp.int32, x.shape, 0)
            for j in range(nh):
                xr = jnp.sum(jnp.where(head == nh * hg + j, x, 0.0), axis=0, keepdims=True) * LOG2E
                c_b[j, ch // per, (ch % per) * LANES:(ch % per + 1) * LANES, :] = jnp.broadcast_to(xr, (LANES, LANES)).T

    q_t = q_ref[0].astype(F32).T.astype(BF16)
    row_head = lax.broadcasted_iota(jnp.int32, q_t.shape, 0) // HD
    q_bd = jnp.concatenate([jnp.where(row_head == j, q_t, jnp.zeros_like(q_t)) for j in range(nh)], axis=1)
    def rows(kt):
        return pl.ds(pl.multiple_of((kt & 1) * tk, tk), tk)

    c_0 = [c_b[j, qi >> 1, rows(qi), :][0:1, :] for j in range(nh)]

    def logits(keys, c_keys):
        bias = jnp.concatenate([c_keys(j) - c_0[j] for j in range(nh) for _ in range(tq // LANES)], axis=1)
        return _nn_dot(keys, q_bd) - bias

    def single(kt):
        return logits(k_r[kt >> 1, rows(kt), :], lambda j: c_b[j, kt >> 1, rows(kt), :])

    def values(v):
        return [v[j * V_AUG:(j + 1) * V_AUG, :] for j in range(nh)]

    def body(k2, carry):
        return _online_update(carry, logits(k_r[k2], lambda j: c_b[j, k2]), values(v_t2[k2]))

    carry = lax.fori_loop(0, qi >> 1, body, _online_init(nh * tq))
    carry = lax.cond((qi & 1) == 1, lambda c: _online_update(c, single(qi - 1), values(v_t[qi - 1])),
                     lambda c: c, carry)
    causal = _mask_bias(lax.broadcasted_iota(jnp.int32, (tk, tq), 0) <= lax.broadcasted_iota(jnp.int32, (tk, tq), 1))
    o_t = _online_finish(_online_update(carry, single(qi) + jnp.concatenate([causal] * nh, axis=1), values(v_t[qi])))
    o = jnp.concatenate([o_t[:, j * tq:(j + 1) * tq] for j in range(nh)], axis=0).T
    o_ref[0] = (o * _silu(gate_ref[0].astype(F32))).astype(o_ref.dtype)


def _fox_prompt(fq, fkv_t, logf_t, fsilu):
    b, t, _ = fq.shape
    tq = ATT_T
    nkt = t // tq
    ng = FOX_H // FOX_HPS
    gw = FOX_HPS * HD
    return pl.pallas_call(
        _fox_prompt_kernel,
        grid=(b, ng, t // tq),
        in_specs=[pl.BlockSpec((1, tq, gw), lambda bi, hg, qi: (bi, qi, hg)),
                  pl.BlockSpec((1, gw, t), lambda bi, hg, qi: (bi, hg, 0)),
                  pl.BlockSpec((1, gw, t), lambda bi, hg, qi: (bi, ng + hg, 0)),
                  pl.BlockSpec((1, FOX_H, t), lambda bi, hg, qi: (bi, 0, 0)),
                  pl.BlockSpec((1, tq, gw), lambda bi, hg, qi: (bi, qi, hg))],
        out_specs=pl.BlockSpec((1, tq, gw), lambda bi, hg, qi: (bi, qi, hg)),
        out_shape=jax.ShapeDtypeStruct((b, t, BR_W), BF16),
        scratch_shapes=[pltpu.VMEM((nkt // 2, 2 * tq, gw), BF16), pltpu.VMEM((nkt, FOX_HPS * V_AUG, tq), BF16),
                        pltpu.VMEM((nkt // 2, FOX_HPS * V_AUG, 2 * tq), BF16),
                        pltpu.VMEM((FOX_HPS, nkt // 2, 2 * tq, LANES), F32)],
        compiler_params=pltpu.CompilerParams(dimension_semantics=("arbitrary",) * 3, vmem_limit_bytes=VMEM_LIMIT),
        name="fox_prompt",
    )(fq, fkv_t, fkv_t, logf_t, fsilu)


def _nsa_prompt_kernel(q_ref, ks_ref, vs_ref, kw_ref, vw_ref, kc_ref, vc_ref, ngate_ref, gate_ref, o_ref,
                       ks_r, vs_t, vs_t2, kw_r, vw_t):
    g, qi = pl.program_id(1), pl.program_id(2)
    tk = ATT_T
    tq = ATT_T
    t = ks_ref.shape[2]
    nkt = t // tk
    nb = t // CMP_BLOCK
    nq = NSA_R * tq

    @pl.when(qi == 0)
    def _():
        for k2 in range(nkt // 2):
            sl = slice(2 * k2 * tk, 2 * (k2 + 1) * tk)
            k_rows = jnp.concatenate([ks_ref[0, :, sl], kw_ref[0, :, sl]], axis=0).T
            ks_r[k2] = k_rows[:, :HD].astype(BF16)
            vs = _augment_values(vs_ref[0, :, sl])
            vw = _augment_values(vw_ref[0, :, sl])
            vs_t2[k2] = vs
            for half in range(2):
                kt = 2 * k2 + half
                kw_r[kt] = k_rows[half * tk:(half + 1) * tk, HD:].astype(BF16)
                vs_t[kt] = vs[:, half * tk:(half + 1) * tk]
                vw_t[kt] = vw[:, half * tk:(half + 1) * tk]

    q_t = q_ref[0].astype(F32).T
    q4 = jnp.concatenate([q_t[r * HD:(r + 1) * HD] for r in range(NSA_R)], axis=1).astype(BF16)
    q0 = qi * tq
    tpos4 = q0 + lax.broadcasted_iota(jnp.int32, (nb, nq), 1) % tq
    blk4 = lax.broadcasted_iota(jnp.int32, (nb, nq), 0)
    readable = (blk4 + 1) * CMP_BLOCK - 1 <= tpos4
    s_c = jnp.where(readable, _nn_dot(kc_ref[0, 0, 0].astype(BF16), q4), NEG)
    m_c = jnp.max(s_c, axis=0, keepdims=True)
    e_c = jnp.where(readable, jnp.exp2(s_c - m_c), 0.0)
    p_c = e_c / jnp.maximum(jnp.sum(e_c, axis=0, keepdims=True), TINY)
    eye = jnp.where(lax.broadcasted_iota(jnp.int32, (HD, HD), 0) == lax.broadcasted_iota(jnp.int32, (HD, HD), 1),
                    1.0, 0.0).astype(BF16)
    vc_t = _nt_dot(eye, vc_ref[0, 0, 0].astype(BF16)).astype(BF16)
    o_c = _nn_dot(vc_t, p_c.astype(BF16))
    imp = p_c[:, 0:tq] + p_c[:, tq:2 * tq] + p_c[:, 2 * tq:3 * tq] + p_c[:, 3 * tq:4 * tq]
    tpos = q0 + lax.broadcasted_iota(jnp.int32, (nb, tq), 1)
    blk = lax.broadcasted_iota(jnp.int32, (nb, tq), 0)
    cur = tpos // CMP_BLOCK
    score = jnp.where(blk < cur, imp, jnp.where(blk == cur, BIG, NEG))
    rank = jnp.zeros((nb, tq), F32)
    for mth in range(nb):
        rowv = score[mth:mth + 1, :]
        ahead = (rowv > score) | ((rowv == score) & (blk > mth))
        rank = rank + jnp.where(ahead, 1.0, 0.0)
    sel_bias = _mask_bias((rank < N_SELECT) & (score > 0.5 * NEG)).astype(BF16)

    key = lax.broadcasted_iota(jnp.int32, (tk, tq), 0)
    qry = lax.broadcasted_iota(jnp.int32, (tk, tq), 1)
    heads = lambda a: jnp.concatenate([a] * NSA_R, axis=1)

    def sel_block_bias(k0, n):
        kb = (k0 + lax.broadcasted_iota(jnp.int32, (n, nb), 0)) // CMP_BLOCK
        expand = jnp.where(kb == lax.broadcasted_iota(jnp.int32, (n, nb), 1), 1.0, 0.0).astype(BF16)
        return _nn_dot(expand, sel_bias)

    def sel_keys(kt):
        return ks_r[kt >> 1, pl.ds(pl.multiple_of((kt & 1) * tk, tk), tk), :]

    def sel_body(k2, carry):
        s = _nn_dot(ks_r[k2], q4) + heads(sel_block_bias(k2 * 2 * tk, 2 * tk))
        return _online_update(carry, s, vs_t2[k2])

    def sel_single(carry):
        kt = qi - 1
        s = _nn_dot(sel_keys(kt), q4) + heads(sel_block_bias(kt * tk, tk))
        return _online_update(carry, s, vs_t[kt])

    carry = lax.fori_loop(0, qi >> 1, sel_body, _online_init(nq))
    carry = lax.cond((qi & 1) == 1, sel_single, lambda c: c, carry)
    s, bias = _nn_dot(sel_keys(qi), q4), sel_block_bias(qi * tk, tk)
    carry = _online_update(carry, s + heads(bias + _mask_bias(key <= qry)), vs_t[qi])
    o_s = _online_finish(carry)
    carry = _online_init(nq)
    k2 = jnp.maximum(qi - 2, 0)
    carry = _online_update(carry, _nn_dot(kw_r[k2], q4) + heads(_mask_bias((key >= qry) & (qi >= 2))), vw_t[k2])
    k1 = jnp.maximum(qi - 1, 0)
    carry = _online_update(carry, _nn_dot(kw_r[k1], q4) + _mask_bias(qi >= 1), vw_t[k1])
    carry = _online_update(carry, _nn_dot(kw_r[qi], q4) + heads(_mask_bias(key <= qry)), vw_t[qi])
    o_w = _online_finish(carry)
    gts = _sigmoid(ngate_ref[0]).T

    def gate_row(br, r):
        c0 = br * NSA_H + r
        return jnp.where(g == 0, gts[c0:c0 + 1, :], gts[c0 + NSA_R:c0 + NSA_R + 1, :])

    outs = []
    for r in range(NSA_R):
        sl = slice(r * tq, (r + 1) * tq)
        outs.append(gate_row(0, r) * o_c[:, sl] + gate_row(1, r) * o_s[:, sl] + gate_row(2, r) * o_w[:, sl])
    o = jnp.concatenate(outs, axis=0).T * _silu(gate_ref[0].astype(F32))
    o_ref[0] = o.astype(o_ref.dtype)


def _nsa_prompt(nq, slc_t, win_t, kcb, vcb, ngate, nsilu):
    b, t, _ = nq.shape
    tq = ATT_T
    nkt = t // tq
    nb = t // CMP_BLOCK
    gw = NSA_R * HD
    kv_spec = lambda kv: pl.BlockSpec((1, HD, t), lambda bi, g, qi, kv=kv: (bi, kv * NSA_KV + g, 0))
    cb_spec = lambda kv: pl.BlockSpec((1, 1, 1, nb, HD), lambda bi, g, qi, kv=kv: (bi, kv, g, 0, 0))
    return pl.pallas_call(
        _nsa_prompt_kernel,
        grid=(b, NSA_KV, t // tq),
        in_specs=[pl.BlockSpec((1, tq, gw), lambda bi, g, qi: (bi, qi, g)),
                  kv_spec(0), kv_spec(1), kv_spec(0), kv_spec(1),
                  cb_spec(0), cb_spec(1),
                  pl.BlockSpec((1, tq, LANES), lambda bi, g, qi: (bi, qi, 0)),
                  pl.BlockSpec((1, tq, gw), lambda bi, g, qi: (bi, qi, g))],
        out_specs=pl.BlockSpec((1, tq, gw), lambda bi, g, qi: (bi, qi, g)),
        out_shape=jax.ShapeDtypeStruct((b, t, BR_W), BF16),
        scratch_shapes=[pltpu.VMEM((nkt // 2, 2 * tq, HD), BF16), pltpu.VMEM((nkt, V_AUG, tq), BF16),
                        pltpu.VMEM((nkt // 2, V_AUG, 2 * tq), BF16),
                        pltpu.VMEM((nkt, tq, HD), BF16), pltpu.VMEM((nkt, V_AUG, tq), BF16)],
        compiler_params=pltpu.CompilerParams(dimension_semantics=("arbitrary",) * 3, vmem_limit_bytes=VMEM_LIMIT),
        name="nsa_prompt",
    )(nq, slc_t, slc_t, win_t, win_t, kcb, vcb, ngate, nsilu)


MEM_SLOTS = 2 * MEM_H


def _mem_kv_kernel(m_ref, g_ref, w_ref, o_ref):
    n_tok = m_ref.shape[1]
    res = _nn_dot(_rmsnorm_rows(m_ref[0], g_ref[...]).astype(BF16), w_ref[...])
    for c in range(MEM_SLOTS):
        o_ref[0, pl.ds(c, n_tok, stride=MEM_SLOTS), :] = res[:, c * MEM_HD:(c + 1) * MEM_HD]


def _mem_kv(mem, g_mem, w_bf):
    b, m, d = mem.shape
    return pl.pallas_call(
        _mem_kv_kernel, grid=(b,),
        in_specs=[pl.BlockSpec((1, m, d), lambda bi: (bi, 0, 0)), pl.BlockSpec((1, d), lambda bi: (0, 0)),
                  pl.BlockSpec((d, MEM_SLOTS * MEM_HD), lambda bi: (0, 0))],
        out_specs=pl.BlockSpec((1, m * MEM_SLOTS, MEM_HD), lambda bi: (bi, 0, 0)),
        out_shape=jax.ShapeDtypeStruct((b, m * MEM_SLOTS, MEM_HD), F32),
        compiler_params=pltpu.CompilerParams(dimension_semantics=("arbitrary",)),
        name="mem_kv",
    )(mem, g_mem, w_bf)


MEM_SCALE = MEM_HD ** -0.5


def _mem_attn_kernel(mq_ref, ms_ref, mkv_ref, o_ref):
    mq = mq_ref[0].astype(BF16)
    n_tok = mkv_ref.shape[1] // MEM_SLOTS
    outs = []
    for h in range(MEM_H):
        kh = mkv_ref[0, pl.ds(h, n_tok, stride=MEM_SLOTS), :].astype(BF16)
        vh = mkv_ref[0, pl.ds(MEM_H + h, n_tok, stride=MEM_SLOTS), :].astype(BF16)
        s = _nt_dot(mq[:, h * MEM_HD:(h + 1) * MEM_HD], kh) * MEM_SCALE
        e = jnp.exp(s - jnp.max(s, axis=1, keepdims=True))
        p = e / jnp.sum(e, axis=1, keepdims=True)
        outs.append(_nn_dot(p.astype(BF16), vh))
    o_ref[0] = (jnp.concatenate(outs, axis=1) * _silu(ms_ref[0].astype(F32))).astype(o_ref.dtype)


MERGE_TM = 512


def _mem_attn(mq, msilu, mkv):
    b, t, _ = mq.shape
    tm = min(MERGE_TM, t)
    rows = mkv.shape[1]
    row = lambda bi, ti: (bi, ti, 0)
    return pl.pallas_call(
        _mem_attn_kernel, grid=(b, t // tm),
        in_specs=[pl.BlockSpec((1, tm, BR_W), row), pl.BlockSpec((1, tm, BR_W), row),
                  pl.BlockSpec((1, rows, MEM_HD), lambda bi, ti: (bi, 0, 0))],
        out_specs=pl.BlockSpec((1, tm, BR_W), row),
        out_shape=jax.ShapeDtypeStruct((b, t, BR_W), BF16),
        compiler_params=pltpu.CompilerParams(dimension_semantics=("arbitrary", "arbitrary")),
        name="mem_attn",
    )(mq, msilu, mkv)


def _merge_kernel(an_ref, af_ref, am_ref, bg_ref, x_ref, wb_ref, wo_ref, gp_ref, o_ref):
    d = wo_ref.shape[0]
    merged = None
    for k, a_ref in enumerate((an_ref, af_ref, am_ref)):
        z = _nn_dot(a_ref[0].astype(BF16), wb_ref[k])
        term = _sigmoid(bg_ref[0, :, k * d:(k + 1) * d].astype(F32)) * z
        merged = term if merged is None else merged + term
    y = _nn_dot(merged.astype(BF16), wo_ref[...])
    o_ref[0] = x_ref[0] + _rmsnorm_rows(y, gp_ref[...])


def _merge(a_nsa, a_fox, a_mem, bgate, x, wb, wo, g_post):
    b, t, d = x.shape
    tm = min(MERGE_TM, t)
    row = lambda bi, ti: (bi, ti, 0)
    resident = dict(pipeline_mode=pl.Buffered(1))
    return pl.pallas_call(
        _merge_kernel, grid=(b, t // tm),
        in_specs=[pl.BlockSpec((1, tm, BR_W), row), pl.BlockSpec((1, tm, BR_W), row), pl.BlockSpec((1, tm, BR_W), row),
                  pl.BlockSpec((1, tm, 3 * d), row), pl.BlockSpec((1, tm, d), row),
                  pl.BlockSpec((3, BR_W, d), lambda bi, ti: (0, 0, 0), **resident),
                  pl.BlockSpec((d, d), lambda bi, ti: (0, 0), **resident),
                  pl.BlockSpec((1, d), lambda bi, ti: (0, 0))],
        out_specs=pl.BlockSpec((1, tm, d), row),
        out_shape=jax.ShapeDtypeStruct((b, t, d), F32),
        compiler_params=pltpu.CompilerParams(dimension_semantics=("arbitrary", "arbitrary"),
                                             vmem_limit_bytes=VMEM_LIMIT),
        name="merge",
    )(a_nsa, a_fox, a_mem, bgate, x, wb, wo, g_post)


_S_ORDER = ("nq", "ncmp", "nslc", "nwin", "nsilu", "fq", "fkv", "fsilu", "mq", "msilu", "bgate", "ngate", "flog")
_S_OFF = {}
_o = 0
for _k in _S_ORDER:
    _S_OFF[_k] = _o
    _o += _COLS[_k][1] - _COLS[_k][0]
assert _o == N_IN and _S_OFF["ngate"] % LANES == 0
N_PAST_SEL = N_SELECT - 1


def _row_to_col(row):
    n = row.shape[1]
    eye = lax.broadcasted_iota(jnp.int32, (n, n), 0) == lax.broadcasted_iota(jnp.int32, (n, n), 1)
    return jnp.sum(jnp.where(eye, row, 0.0), axis=1, keepdims=True)


def _col_to_row(col):
    n = col.shape[0]
    eye = lax.broadcasted_iota(jnp.int32, (n, n), 0) == lax.broadcasted_iota(jnp.int32, (n, n), 1)
    return jnp.sum(jnp.where(eye, col, 0.0), axis=0, keepdims=True)


def _hrow(hs_ref, key, off=0, n=None):
    lo = _S_OFF[key] + off
    n = (_COLS[key][1] - _COLS[key][0] - off) if n is None else n
    return hs_ref[0, :, lo:lo + n]


def _q_group(hs_ref, key, g):
    return jnp.concatenate([_hrow(hs_ref, key, (g * NSA_R + r) * HD, HD) for r in range(NSA_R)], axis=0) * SCALE


def _nsa_sample_sel_kernel(hs_ref, cb_ref, oc_ref, idx_ref):
    nblk = cb_ref.shape[3]
    idx_ref[...] = jnp.zeros(idx_ref.shape, idx_ref.dtype)
    for g in range(NSA_KV):
        q4 = _q_group(hs_ref, "nq", g).astype(BF16)
        s = _nt_dot(q4, cb_ref[0, 0, g].astype(BF16))
        e = jnp.exp(s - jnp.max(s, axis=1, keepdims=True))
        p = e / jnp.sum(e, axis=1, keepdims=True)
        oc_ref[0, g * NSA_R:(g + 1) * NSA_R, :] = _nn_dot(p.astype(BF16), cb_ref[0, 1, g].astype(BF16))
        imp = jnp.sum(p, axis=0, keepdims=True)
        col = _row_to_col(imp)
        mi = lax.broadcasted_iota(jnp.int32, (nblk, nblk), 0)
        ni = lax.broadcasted_iota(jnp.int32, (nblk, nblk), 1)
        ahead = (col > imp) | ((col == imp) & (mi < ni))
        rank = jnp.sum(jnp.where(ahead, 1.0, 0.0), axis=0, keepdims=True)
        sel = jnp.where(rank < N_PAST_SEL, 1.0, 0.0)
        pos, carry = [], jnp.zeros((1, 1), F32)
        for ch in range(nblk // LANES):
            chunk = sel[:, ch * LANES:(ch + 1) * LANES]
            incl = _lane_cumsum(chunk) + carry
            carry = incl[:, LANES - 1:LANES]
            pos.append(incl - chunk)
        pos = jnp.concatenate(pos, axis=1)
        slot = lax.broadcasted_iota(jnp.int32, (N_SELECT, nblk), 0).astype(F32)
        onehot = jnp.where((pos == slot) & (sel > 0.5), 1.0, 0.0).astype(BF16)
        ids = lax.broadcasted_iota(jnp.int32, (1, nblk), 1).astype(F32).astype(BF16)
        idx = _nt_dot(ids, onehot)
        idx_ref[0, g:g + 1, 0:N_SELECT] = idx.astype(jnp.int32)


def _nsa_sample_sel(hs3, cb):
    n = hs3.shape[0]
    nblk = cb.shape[3]
    assert nblk % LANES == 0 and nblk >= N_PAST_SEL and nblk <= 256
    return pl.pallas_call(
        _nsa_sample_sel_kernel, grid=(n,),
        in_specs=[pl.BlockSpec((1, 1, N_IN_PAD), lambda b: (b, 0, 0)),
                  pl.BlockSpec((1, 2, NSA_KV, nblk, HD), lambda b: (b, 0, 0, 0, 0))],
        out_specs=[pl.BlockSpec((1, NSA_H, HD), lambda b: (b, 0, 0)),
                   pl.BlockSpec((1, NSA_KV, LANES), lambda b: (b, 0, 0))],
        out_shape=[jax.ShapeDtypeStruct((n, NSA_H, HD), F32), jax.ShapeDtypeStruct((n, NSA_KV, LANES), jnp.int32)],
        compiler_params=pltpu.CompilerParams(dimension_semantics=("arbitrary",)),
        name="nsa_sample_select",
    )(hs3, cb)


def _nsa_sample_attn_kernel(idx_ref, pt_ref, hs_ref, oc_ref, win_ref, *refs):
    n_blk = NSA_KV * N_PAST_SEL
    blk_refs, (o_ref, nw_ref) = refs[:n_blk], refs[n_blk:]
    b = pl.program_id(0)
    gates = _sigmoid(_hrow(hs_ref, "ngate"))
    lane_half = lax.broadcasted_iota(jnp.int32, (NSA_R, PAGE_SIZE), 1) // CMP_BLOCK
    outs = []
    for g in range(NSA_KV):
        q4 = _q_group(hs_ref, "nq", g)
        q4b = q4.astype(BF16)
        kn, vn = _hrow(hs_ref, "nslc", g * HD, HD), _hrow(hs_ref, "nslc", KV_W + g * HD, HD)
        s_cur = jnp.sum(q4 * kn, axis=1, keepdims=True)
        s_list, masks = [], []
        for k in range(N_PAST_SEL):
            half = idx_ref[b, g * LANES + k] & 1
            mask = lane_half == half
            s = _nn_dot(q4b, blk_refs[g * N_PAST_SEL + k][0, 0, 0].astype(BF16))
            s_list.append(jnp.where(mask, s, NEG))
            masks.append(mask)
        m = s_cur
        for s in s_list:
            m = jnp.maximum(m, jnp.max(s, axis=1, keepdims=True))
        l = jnp.exp(s_cur - m)
        acc = l * vn
        for k in range(N_PAST_SEL):
            e = jnp.where(masks[k], jnp.exp(s_list[k] - m), 0.0)
            l = l + jnp.sum(e, axis=1, keepdims=True)
            acc = acc + _nt_dot(e.astype(BF16), blk_refs[g * N_PAST_SEL + k][0, 1, 0].astype(BF16))
        o_s = acc / jnp.maximum(l, TINY)
        kwn, vwn = _hrow(hs_ref, "nwin", g * HD, HD), _hrow(hs_ref, "nwin", KV_W + g * HD, HD)
        s_w = _nn_dot(q4b, win_ref[0, 0, g].astype(BF16))
        s_wn = jnp.sum(q4 * kwn, axis=1, keepdims=True)
        m = jnp.maximum(jnp.max(s_w, axis=1, keepdims=True), s_wn)
        e, en = jnp.exp(s_w - m), jnp.exp(s_wn - m)
        l = jnp.sum(e, axis=1, keepdims=True) + en
        o_w = (_nt_dot(e.astype(BF16), win_ref[0, 1, g].astype(BF16)) + en * vwn) / jnp.maximum(l, TINY)
        o_c = oc_ref[0, g * NSA_R:(g + 1) * NSA_R, :]
        for r in range(NSA_R):
            c = g * NSA_R + r
            gate = lambda br: gates[:, br * NSA_H + c: br * NSA_H + c + 1]
            outs.append(gate(0) * o_c[r:r + 1] + gate(1) * o_s[r:r + 1] + gate(2) * o_w[r:r + 1])
    o_ref[0] = jnp.concatenate(outs, axis=1) * _silu(_hrow(hs_ref, "nsilu"))
    wb = win_ref.shape[4]
    newcol = _row_to_col(_hrow(hs_ref, "nwin"))
    last = lax.broadcasted_iota(jnp.int32, (HD, wb), 1) == wb - 1
    for kv in range(2):
        for g in range(NSA_KV):
            r0 = (kv * NSA_KV + g) * HD
            shifted = pltpu.roll(win_ref[0, kv, g], wb - 1, axis=1)
            nw_ref[0, kv, g] = jnp.where(last, newcol[r0:r0 + HD], shifted)


def _nsa_sample_attn(idx, page_table, hs3, o_c, win_state, pool_slc):
    n = hs3.shape[0]
    wb = win_state.shape[4]
    assert wb == WINDOW
    blk_specs = [pl.BlockSpec((1, 2, 1, HD, PAGE_SIZE),
                              lambda b, idx, pt, g=g, k=k: (pt[b, idx[b, g * LANES + k] >> 1], 0, g, 0, 0))
                 for g in range(NSA_KV) for k in range(N_PAST_SEL)]
    grid_spec = pltpu.PrefetchScalarGridSpec(
        num_scalar_prefetch=2, grid=(n,),
        in_specs=[pl.BlockSpec((1, 1, N_IN_PAD), lambda b, idx, pt: (b, 0, 0)),
                  pl.BlockSpec((1, NSA_H, HD), lambda b, idx, pt: (b, 0, 0)),
                  pl.BlockSpec((1, 2, NSA_KV, HD, wb), lambda b, idx, pt: (b, 0, 0, 0, 0))] + blk_specs,
        out_specs=[pl.BlockSpec((1, 1, BR_W), lambda b, idx, pt: (b, 0, 0)),
                   pl.BlockSpec((1, 2, NSA_KV, HD, wb), lambda b, idx, pt: (b, 0, 0, 0, 0))],
    )
    return pl.pallas_call(
        _nsa_sample_attn_kernel, grid_spec=grid_spec,
        out_shape=[jax.ShapeDtypeStruct((n, 1, BR_W), F32), jax.ShapeDtypeStruct((n, 2, NSA_KV, HD, wb), F32)],
        compiler_params=pltpu.CompilerParams(dimension_semantics=("arbitrary",)),
        name="nsa_sample_attn",
    )(idx, page_table, hs3, o_c, win_state, *([pool_slc] * (NSA_KV * N_PAST_SEL)))


FOX_NPG = 16


def _fox_sample_kernel(pt_ref, hs_ref, *refs):
    kv_refs = refs[:FOX_NPG]
    lf_hbm, o_ref, q_b, acc, m_s, l_s, r_s, lf_buf, lf_sem = refs[FOX_NPG:]
    n_pages = r_s.shape[0]
    b, j = pl.program_id(0), pl.program_id(1)

    def lf_copy(b_, p, slot_):
        return pltpu.make_async_copy(lf_hbm.at[pt_ref[b_, p]], lf_buf.at[slot_, p], lf_sem.at[slot_])

    @pl.when(j == 0)
    def _():
        slot = b % 2

        @pl.when(b == 0)
        def _():
            for p in range(n_pages):
                lf_copy(b, p, slot).start()

        @pl.when(b + 1 < pl.num_programs(0))
        def _():
            for p in range(n_pages):
                lf_copy(b + 1, p, 1 - slot).start()

        for h in range(FOX_H):
            q_b[h] = jnp.broadcast_to(_row_to_col(_hrow(hs_ref, "fq", h * HD, HD) * SCALE), (HD, PAGE_SIZE))
        acc[...] = jnp.zeros(acc.shape, F32)
        m_s[...] = jnp.full(m_s.shape, NEG, F32)
        l_s[...] = jnp.zeros(l_s.shape, F32)
        for p in range(n_pages):
            lf_copy(0, p, slot).wait()
        incl = _lane_cumsum(lf_buf[slot].reshape(n_pages * FOX_H, PAGE_SIZE))
        later = _row_to_col(_hrow(hs_ref, "flog"))
        for p in reversed(range(n_pages)):
            inc = incl[p * FOX_H:(p + 1) * FOX_H]
            total = inc[:, PAGE_SIZE - 1:PAGE_SIZE]
            r_s[p] = later + (total - inc)
            later = later + total

    s_list = []
    for i in range(FOX_NPG):
        qk = jnp.concatenate([jnp.sum(kv_refs[i][0, 0, h] * q_b[h], axis=0, keepdims=True) for h in range(FOX_H)], axis=0)
        s_list.append(qk + r_s[j * FOX_NPG + i])
    s_max = s_list[0]
    for s in s_list[1:]:
        s_max = jnp.maximum(s_max, s)
    m_old = m_s[...]
    m_new = jnp.maximum(m_old, jnp.max(s_max, axis=1, keepdims=True))
    alpha = jnp.exp(m_old - m_new)
    p_list = [jnp.exp(s - m_new) for s in s_list]
    p_sum = p_list[0]
    for p in p_list[1:]:
        p_sum = p_sum + p
    m_s[...] = m_new
    l_s[...] = l_s[...] * alpha + jnp.sum(p_sum, axis=1, keepdims=True)
    for h in range(FOX_H):
        a = acc[h] * alpha[h:h + 1, :]
        for i in range(FOX_NPG):
            a = a + kv_refs[i][0, 1, h] * p_list[i][h:h + 1, :]
        acc[h] = a

    @pl.when(j == pl.num_programs(1) - 1)
    def _():
        outs = []
        for h in range(FOX_H):
            q = _hrow(hs_ref, "fq", h * HD, HD) * SCALE
            kn, vn = _hrow(hs_ref, "fkv", h * HD, HD), _hrow(hs_ref, "fkv", FOX_H * HD + h * HD, HD)
            s_n = jnp.sum(q * kn, axis=1, keepdims=True)
            m_h, l_h = m_s[h:h + 1, :], l_s[h:h + 1, :]
            m_f = jnp.maximum(m_h, s_n)
            al, pn = jnp.exp(m_h - m_f), jnp.exp(s_n - m_f)
            past = _col_to_row(jnp.sum(acc[h], axis=1, keepdims=True))
            outs.append((past * al + pn * vn) / jnp.maximum(l_h * al + pn, TINY))
        o_ref[0] = jnp.concatenate(outs, axis=1) * _silu(_hrow(hs_ref, "fsilu"))


def _fox_sample(page_table, hs3, pool_kv, pool_lf):
    n, n_pages = page_table.shape
    assert n_pages % FOX_NPG == 0
    ng = n_pages // FOX_NPG
    kv_specs = [pl.BlockSpec((1, 2, FOX_H, HD, PAGE_SIZE), lambda b, j, pt, i=i: (pt[b, j * FOX_NPG + i], 0, 0, 0, 0))
                for i in range(FOX_NPG)]
    grid_spec = pltpu.PrefetchScalarGridSpec(
        num_scalar_prefetch=1, grid=(n, ng),
        in_specs=[pl.BlockSpec((1, 1, N_IN_PAD), lambda b, j, pt: (b, 0, 0))] + kv_specs
                 + [pl.BlockSpec(memory_space=pl.ANY)],
        out_specs=pl.BlockSpec((1, 1, BR_W), lambda b, j, pt: (b, 0, 0)),
        scratch_shapes=[pltpu.VMEM((FOX_H, HD, PAGE_SIZE), F32), pltpu.VMEM((FOX_H, HD, PAGE_SIZE), F32),
                        pltpu.VMEM((FOX_H, 1), F32), pltpu.VMEM((FOX_H, 1), F32),
                        pltpu.VMEM((n_pages, FOX_H, PAGE_SIZE), F32),
                        pltpu.VMEM((2, n_pages, FOX_H, PAGE_SIZE), F32), pltpu.SemaphoreType.DMA((2,))],
    )
    return pl.pallas_call(
        _fox_sample_kernel, grid_spec=grid_spec,
        out_shape=jax.ShapeDtypeStruct((n, 1, BR_W), F32),
        compiler_params=pltpu.CompilerParams(dimension_semantics=("arbitrary", "arbitrary"),
                                             vmem_limit_bytes=VMEM_LIMIT),
        name="fox_sample",
    )(page_table, hs3, *([pool_kv] * FOX_NPG), pool_lf)


def _split_weights(w_in):
    wt = w_in.T
    sl = lambda k: wt[_COLS[k][0]:_COLS[k][1]]
    d = wt.shape[1]
    wq = jnp.concatenate([sl(k) for k in _QGROUPS] + [sl("bgate"), sl("ngate"),
                                                        jnp.zeros((LANES - 24, d), wt.dtype)], axis=0).astype(BF16)
    wkv = jnp.concatenate([sl(k) for k, _ in _KVGROUPS], axis=0).astype(BF16)
    w_all = jnp.concatenate([sl(k) for k in _S_ORDER] + [jnp.zeros((N_IN_PAD - N_IN, d), wt.dtype)],
                            axis=0).astype(BF16)
    return wq, wkv, w_all


def kernel(x_prompt, x_sample, cache_nsa_cmp, cache_nsa_slc, state_nsa_win, cache_fox_kv, cache_fox_logf, cache_mem_kv, page_table, mem_prompt, g_pre, g_post, g_mem, w_in, b_fgate, w_mem_kv, w_cmp1, w_cmp2, pe_cmp, w_branch, w_out):
    assert w_in.shape[0] == 1 and x_sample.shape[1] == 1, "one layer, one new token per sequence"
    b, t, d = x_prompt.shape
    n_s, n_pages = page_table.shape
    past = n_pages * PAGE_SIZE
    wq, wkv, w_all = _split_weights(w_in[0])
    w1bd, w2bd, pe_t = _compress_weights(w_cmp1[0], w_cmp2[0], pe_cmp[0])
    wb, wo = w_branch[0].astype(BF16), w_out[0].astype(BF16)

    (nq, nsilu, fq, fsilu, mq, msilu, bgate, ngate, cmp_t, slc_t, win_t, fkv_t, flog_t) = _proj_prompt(
        x_prompt, g_pre, wq, wkv, _rope_tables(jnp.arange(t, dtype=F32)), b_fgate[0])
    cb_p = _compress(cmp_t.reshape(b, 2, NSA_KV, HD, t), jnp.zeros((1, 1), jnp.int32), w1bd, w2bd, pe_t,
                     n_batch=b, n_pages=t // PAGE_SIZE, paged=False)
    cb_p = cb_p.reshape(b, 2, NSA_KV, t // CMP_BLOCK, HD)
    a_nsa = _nsa_prompt(nq, slc_t, win_t, cb_p, cb_p, ngate, nsilu)
    a_fox = _fox_prompt(fq, fkv_t, flog_t, fsilu)
    mkv_p = _mem_kv(mem_prompt, g_mem, w_mem_kv[0].astype(BF16))
    a_mem = _mem_attn(mq, msilu, mkv_p)
    y_p = _merge(a_nsa, a_fox, a_mem, bgate, x_prompt, wb, wo, g_post)

    rows_minor = lambda a: jnp.transpose(a, (0, 2, 3, 4, 1))
    hs = _proj_sample(x_sample[:, 0, :], g_pre, w_all, _rope_tables(jnp.full((n_s,), past, F32)), b_fgate[0])
    hs3 = hs.reshape(n_s, 1, N_IN_PAD)
    cb_s = _compress(rows_minor(cache_nsa_cmp[0]), page_table, w1bd, w2bd, pe_t,
                     n_batch=n_s, n_pages=n_pages, paged=True)
    cb_s = cb_s.reshape(n_s, 2, NSA_KV, past // CMP_BLOCK, HD)
    o_c, idx = _nsa_sample_sel(hs3, cb_s)
    a_nsa_s, new_win = _nsa_sample_attn(idx.reshape(n_s, NSA_KV * LANES), page_table, hs3, o_c,
                                        rows_minor(state_nsa_win[0]), rows_minor(cache_nsa_slc[0]))
    a_fox_s = _fox_sample(page_table, hs3, rows_minor(cache_fox_kv[0]),
                          jnp.transpose(cache_fox_logf[0], (0, 2, 1)))
    col = lambda k: hs[:, _S_OFF[k]:_S_OFF[k] + _COLS[k][1] - _COLS[k][0]]
    a_mem_s = _mem_attn(col("mq").reshape(n_s, 1, BR_W), col("msilu").reshape(n_s, 1, BR_W),
                        cache_mem_kv[0].reshape(n_s, -1, MEM_HD))
    y_s = _merge(a_nsa_s.reshape(1, n_s, BR_W), a_fox_s.reshape(1, n_s, BR_W), a_mem_s.reshape(1, n_s, BR_W),
                 col("bgate").reshape(1, n_s, 3 * d), x_sample.reshape(1, n_s, d), wb, wo, g_post)

    def rows_major(a_t, heads):
        n, _, r = a_t.shape
        return jnp.transpose(a_t.reshape(n, 2, heads, HD, r), (0, 4, 1, 2, 3))[None]

    wkeep = min(WINDOW, t)
    return (y_p, y_s.reshape(n_s, 1, d),
            rows_major(cmp_t, NSA_KV), rows_major(slc_t, NSA_KV), rows_major(win_t[:, :, t - wkeep:], NSA_KV),
            rows_major(fkv_t, FOX_H), jnp.transpose(flog_t, (0, 2, 1))[None],
            mkv_p.reshape(1, b, -1, 2, MEM_H, MEM_HD),
            col("ncmp").reshape(1, n_s, 1, 2, NSA_KV, HD), col("nslc").reshape(1, n_s, 1, 2, NSA_KV, HD),
            jnp.transpose(new_win, (0, 4, 1, 2, 3))[None],
            col("fkv").reshape(1, n_s, 1, 2, FOX_H, HD), col("flog").reshape(1, n_s, 1, FOX_H))
```
